```python
import jax, jax.numpy as jnp
from jax import lax
import numpy as np

D_MODEL = 1024
BATCH = 16
SEQ = 2048
DEPTH = 1
DEC_BATCH = 32
DEC_SEQ = 4
PAST_LEN = 16384
PAGE_SIZE = 128

N_HEADS = 8
HEAD_DIM = 64
ATT_WIDTH = N_HEADS * HEAD_DIM
CONV_CH = D_MODEL // 2
CONV_WIDTH = 31
FFN_HIDDEN = -(-8 * D_MODEL // (3 * 256)) * 256
PLE_DIM = 256
Q_BLOCK = 128
LN_EPS = 1e-5
SB_BIAS_INIT = -6.0
DEEPNORM_ALPHA = (2.0 * DEPTH) ** 0.25
DEEPNORM_BETA = (8.0 * DEPTH) ** -0.25
IN_SPLITS = (CONV_CH, CONV_CH, ATT_WIDTH, ATT_WIDTH, ATT_WIDTH, D_MODEL, D_MODEL)
IN_COLS = sum(IN_SPLITS)

kernel_name = 'hybrid_conformer_conv_stickbreaking_decoder_step'


def layer_norm(x, g, b):
    xf = x.astype(jnp.float32)
    mu = jnp.mean(xf, axis=-1, keepdims=True)
    xc = xf - mu
    var = jnp.mean(xc * xc, axis=-1, keepdims=True)
    return (xc * lax.rsqrt(var + LN_EPS) * g + b).astype(x.dtype)


def stick_breaking_block(q, k, v, sb_bias, qpos, kpos):
    z = jnp.einsum('bqhd,bshd->bhqs', q, k, preferred_element_type=jnp.float32) * (HEAD_DIM ** -0.5)
    z = z + sb_bias.astype(jnp.float32)[None, :, None, None]
    mask = kpos[None, :] < qpos[:, None]
    log_1m = jnp.where(mask, jax.nn.log_sigmoid(-z), 0.0)
    between = lax.cumsum(log_1m, axis=3, reverse=True) - log_1m
    a = jnp.where(mask, jnp.exp(jax.nn.log_sigmoid(z) + between), 0.0)
    return jnp.einsum('bhqs,bshd->bqhd', a.astype(v.dtype), v)


def stick_breaking_attention(q, k, v, sb_bias, q_offset):
    b, tq = q.shape[0], q.shape[1]
    blk = Q_BLOCK if tq % Q_BLOCK == 0 else tq
    nb = tq // blk
    kpos = jnp.arange(k.shape[1], dtype=jnp.int32)
    qpos = (q_offset + jnp.arange(tq, dtype=jnp.int32)).reshape(nb, blk)
    qb = q.reshape(b, nb, blk, N_HEADS, HEAD_DIM).transpose(1, 0, 2, 3, 4)
    out = lax.map(lambda a: stick_breaking_block(a[0], k, v, sb_bias, a[1], kpos), (qb, qpos))
    return out.transpose(1, 0, 2, 3, 4).reshape(b, tq, N_HEADS, HEAD_DIM)


def causal_depthwise_conv(u, past, conv_w, conv_b):
    full = jnp.concatenate([past, u], axis=1)
    out = lax.conv_general_dilated(full, conv_w[:, None, :], window_strides=(1,), padding='VALID',
                                   dimension_numbers=('NWC', 'WIO', 'NWC'),
                                   feature_group_count=CONV_CH)
    return out + conv_b, full[:, full.shape[1] - (CONV_WIDTH - 1):]


def decoder_layer(x, p, past_k, past_v, past_conv, w_in, sb_bias, conv_w, conv_b, conv_ln_g,
                  conv_ln_b, w_conv_proj, w_att_proj, w_out, ln1_g, ln1_b, w_ffn_up, w_ffn_down,
                  w_ple_gate, w_ple, ln2_g, ln2_b):
    b, t = x.shape[0], x.shape[1]
    cuts = np.cumsum(IN_SPLITS)[:-1].tolist()
    glu_a, glu_b, q, k, v, g_conv, g_att = jnp.split(x @ w_in, cuts, axis=-1)
    u = glu_a * jax.nn.sigmoid(glu_b)
    c, new_conv = causal_depthwise_conv(u, past_conv, conv_w, conv_b)
    conv_out = jax.nn.silu(layer_norm(c, conv_ln_g, conv_ln_b)) @ w_conv_proj
    q = q.reshape(b, t, N_HEADS, HEAD_DIM)
    k = k.reshape(b, t, N_HEADS, HEAD_DIM)
    v = v.reshape(b, t, N_HEADS, HEAD_DIM)
    k_all = jnp.concatenate([past_k, k], axis=1)
    v_all = jnp.concatenate([past_v, v], axis=1)
    o = stick_breaking_attention(q, k_all, v_all, sb_bias, past_k.shape[1])
    att_out = o.reshape(b, t, ATT_WIDTH) @ w_att_proj
    mixed = (jax.nn.sigmoid(g_conv) * conv_out + jax.nn.sigmoid(g_att) * att_out) @ w_out
    x = layer_norm(DEEPNORM_ALPHA * x + mixed, ln1_g, ln1_b)
    gate, up = jnp.split(x @ w_ffn_up, 2, axis=-1)
    ffn = (jax.nn.silu(gate) * up) @ w_ffn_down
    ple = jax.nn.sigmoid(x @ w_ple_gate) * (p @ w_ple)
    x = layer_norm(DEEPNORM_ALPHA * x + ffn + ple, ln2_g, ln2_b)
    return x, k, v, new_conv


def setup_inputs(seed: int = 0) -> dict:
    key = jax.random.key(seed)
    ks = jax.random.split(key, 28)
    n_pages = PAST_LEN // PAGE_SIZE
    n_used = DEC_BATCH * n_pages
    n_pool = n_used + n_used // 4
    f32 = jnp.float32
    nrm = lambda i, shape, s: jax.random.normal(ks[i], shape, f32) * s
    page_table = jax.random.permutation(ks[27], n_pool)[:n_used].reshape(DEC_BATCH, n_pages).astype(jnp.int32)
    return {
        'x_prompt': nrm(0, (BATCH, SEQ, D_MODEL), 1.0),
        'x_sample': nrm(1, (DEC_BATCH, DEC_SEQ, D_MODEL), 1.0),
        'p_prompt': nrm(2, (DEPTH, BATCH, SEQ, PLE_DIM), 1.0),
        'p_sample': nrm(3, (DEPTH, DEC_BATCH, DEC_SEQ, PLE_DIM), 1.0),
        'cache_k': nrm(4, (DEPTH, n_pool, PAGE_SIZE, N_HEADS, HEAD_DIM), 1.0),
        'cache_v': nrm(5, (DEPTH, n_pool, PAGE_SIZE, N_HEADS, HEAD_DIM), 1.0),
        'state_conv': nrm(6, (DEPTH, DEC_BATCH, CONV_WIDTH - 1, CONV_CH), 0.5),
        'page_table': page_table,
        'w_in': nrm(7, (DEPTH, D_MODEL, IN_COLS), D_MODEL ** -0.5),
        'sb_bias': SB_BIAS_INIT + nrm(23, (DEPTH, N_HEADS), 0.1),
        'conv_w': nrm(8, (DEPTH, CONV_WIDTH, CONV_CH), CONV_WIDTH ** -0.5),
        'conv_b': nrm(9, (DEPTH, CONV_CH), 0.01),
        'conv_ln_g': 1.0 + nrm(10, (DEPTH, CONV_CH), 0.01),
        'conv_ln_b': nrm(11, (DEPTH, CONV_CH), 0.01),
        'w_conv_proj': nrm(12, (DEPTH, CONV_CH, D_MODEL), CONV_CH ** -0.5),
        'w_att_proj': nrm(13, (DEPTH, ATT_WIDTH, D_MODEL), ATT_WIDTH ** -0.5),
        'w_out': nrm(14, (DEPTH, D_MODEL, D_MODEL), D_MODEL ** -0.5 * DEEPNORM_BETA),
        'ln1_g': 1.0 + nrm(15, (DEPTH, D_MODEL), 0.01),
        'ln1_b': nrm(16, (DEPTH, D_MODEL), 0.01),
        'w_ffn_up': nrm(17, (DEPTH, D_MODEL, 2 * FFN_HIDDEN), D_MODEL ** -0.5),
        'w_ffn_down': nrm(18, (DEPTH, FFN_HIDDEN, D_MODEL), FFN_HIDDEN ** -0.5 * DEEPNORM_BETA),
        'w_ple_gate': nrm(19, (DEPTH, D_MODEL, D_MODEL), D_MODEL ** -0.5),
        'w_ple': nrm(20, (DEPTH, PLE_DIM, D_MODEL), PLE_DIM ** -0.5),
        'ln2_g': 1.0 + nrm(21, (DEPTH, D_MODEL), 0.01),
        'ln2_b': nrm(22, (DEPTH, D_MODEL), 0.01),
    }


def reference(x_prompt, x_sample, p_prompt, p_sample, cache_k, cache_v, state_conv, page_table,
              w_in, sb_bias, conv_w, conv_b, conv_ln_g, conv_ln_b, w_conv_proj, w_att_proj, w_out,
              ln1_g, ln1_b, w_ffn_up, w_ffn_down, w_ple_gate, w_ple, ln2_g, ln2_b):
    y_p, y_s = x_prompt, x_sample
    bp, bs = x_prompt.shape[0], x_sample.shape[0]
    kp_l, vp_l, cp_l, ks_l, vs_l, cs_l = [], [], [], [], [], []
    for i in range(DEPTH):
        lw = (w_in[i], sb_bias[i], conv_w[i], conv_b[i], conv_ln_g[i], conv_ln_b[i], w_conv_proj[i],
              w_att_proj[i], w_out[i], ln1_g[i], ln1_b[i], w_ffn_up[i], w_ffn_down[i],
              w_ple_gate[i], w_ple[i], ln2_g[i], ln2_b[i])
        empty_kv = jnp.zeros((bp, 0, N_HEADS, HEAD_DIM), x_prompt.dtype)
        zero_conv = jnp.zeros((bp, CONV_WIDTH - 1, CONV_CH), x_prompt.dtype)
        y_p, k1, v1, c1 = decoder_layer(y_p, p_prompt[i], empty_kv, empty_kv, zero_conv, *lw)
        past_k = cache_k[i][page_table].reshape(bs, -1, N_HEADS, HEAD_DIM)
        past_v = cache_v[i][page_table].reshape(bs, -1, N_HEADS, HEAD_DIM)
        y_s, k2, v2, c2 = decoder_layer(y_s, p_sample[i], past_k, past_v, state_conv[i], *lw)
        kp_l.append(k1); vp_l.append(v1); cp_l.append(c1)
        ks_l.append(k2); vs_l.append(v2); cs_l.append(c2)
    k_prompt = jnp.stack(kp_l)
    v_prompt = jnp.stack(vp_l)
    conv_prompt = jnp.stack(cp_l)
    k_sample = jnp.stack(ks_l)
    v_sample = jnp.stack(vs_l)
    conv_sample = jnp.stack(cs_l)
    return (y_p, y_s, k_prompt, v_prompt, conv_prompt, k_sample, v_sample, conv_sample)
```

```python
import functools

import jax
import jax.numpy as jnp
from jax import lax
from jax.experimental import pallas as pl
from jax.experimental.pallas import tpu as pltpu

F32 = jnp.float32
BF16 = jnp.bfloat16

N_HEADS = 8
HEAD_DIM = 64
ATT_WIDTH = N_HEADS * HEAD_DIM
CONV_WIDTH = 31
HIST = CONV_WIDTH - 1
HIST_PAD = 32
LN_EPS = 1e-5
LANES = 128
HEADS_PER_LANE_TILE = LANES // HEAD_DIM
VMEM_LIMIT_BYTES = 56 * 1024 * 1024
NEG_BIG = -1e30


def _layer_norm(x, g, b):
    mu = jnp.mean(x, axis=-1, keepdims=True)
    xc = x - mu
    var = jnp.mean(xc * xc, axis=-1, keepdims=True)
    return xc * lax.rsqrt(var + LN_EPS) * g + b


def _neg_softplus(z):
    return -(jnp.maximum(z, 0.0) + jnp.log1p(jnp.exp(-jnp.abs(z))))


def _split_bf16(x):
    hi = x.astype(BF16)
    lo = (x - hi.astype(F32)).astype(BF16)
    return hi, lo


def _tri_ge(n):
    j = lax.broadcasted_iota(jnp.int32, (n, n), 0)
    s = lax.broadcasted_iota(jnp.int32, (n, n), 1)
    return (j >= s).astype(BF16)


def _rev_cumsum(l, tri):
    hi, lo = _split_bf16(l)
    return (jnp.dot(hi, tri, preferred_element_type=F32)
            + jnp.dot(lo, tri, preferred_element_type=F32))


def _const_spec(shape):
    nd = len(shape)
    return pl.BlockSpec(shape, lambda *_: (0,) * nd, pipeline_mode=pl.Buffered(1))


def _inproj_conv_body(x_ref, w_ref, past_ref, cw_ref, cb_ref, lg_ref, lb_ref,
                      cact_ref, q_ref, k_ref, v_ref, kb_ref, vb_ref, nc_ref, ubuf,
                      *, tm, nt, valid_last, conv_ch):
    t = pl.program_id(1)

    @pl.when(t == 0)
    def _():
        ubuf[pl.ds(0, HIST_PAD), :] = past_ref[0]

    xb = x_ref[0].astype(BF16)
    h = jnp.dot(xb, w_ref[...], preferred_element_type=F32)
    c = conv_ch
    u = h[:, :c] * jax.nn.sigmoid(h[:, c:2 * c])
    q = h[:, 2 * c:2 * c + ATT_WIDTH]
    k = h[:, 2 * c + ATT_WIDTH:2 * c + 2 * ATT_WIDTH]
    v = h[:, 2 * c + 2 * ATT_WIDTH:2 * c + 3 * ATT_WIDTH]
    q_ref[0] = (q * (HEAD_DIM ** -0.5)).astype(BF16)
    k_ref[0] = k
    v_ref[0] = v
    kb_ref[0] = k.astype(BF16)
    vb_ref[0] = v.astype(BF16)

    ubuf[pl.ds(HIST_PAD, tm), :] = u
    acc = jnp.zeros((tm, c), F32) + cb_ref[...]
    for kk in range(CONV_WIDTH):
        acc = acc + ubuf[pl.ds(HIST_PAD - HIST + kk, tm), :] * cw_ref[pl.ds(kk, 1), :]
    y = _layer_norm(acc, lg_ref[...], lb_ref[...])
    cact_ref[0] = (y * jax.nn.sigmoid(y)).astype(BF16)

    @pl.when(t == nt - 1)
    def _():
        nc_ref[0] = ubuf[pl.ds(HIST_PAD + valid_last - HIST, HIST), :]

    if nt > 1:
        ubuf[pl.ds(0, HIST_PAD), :] = ubuf[pl.ds(tm, HIST_PAD), :]


def _inproj_conv(x, w_a, past_pad, conv_w, conv_b, ln_g, ln_b, *, tm, valid_last):
    b, t, d = x.shape
    c = conv_w.shape[1]
    nt = t // tm
    assert nt * tm == t and tm % 8 == 0
    body = functools.partial(_inproj_conv_body, tm=tm, nt=nt, valid_last=valid_last, conv_ch=c)
    tile = lambda w: pl.BlockSpec((1, tm, w), lambda i, j: (i, j, 0))
    per_b = lambda r, w: pl.BlockSpec((1, r, w), lambda i, j: (i, 0, 0))
    return pl.pallas_call(
        body,
        grid=(b, nt),
        in_specs=[tile(d), _const_spec(w_a.shape), per_b(HIST_PAD, c),
                  _const_spec(conv_w.shape), _const_spec((1, c)), _const_spec((1, c)),
                  _const_spec((1, c))],
        out_specs=[tile(c), tile(ATT_WIDTH), tile(ATT_WIDTH), tile(ATT_WIDTH), tile(ATT_WIDTH),
                   tile(ATT_WIDTH), per_b(HIST, c)],
        out_shape=[jax.ShapeDtypeStruct((b, t, c), BF16),
                   jax.ShapeDtypeStruct((b, t, ATT_WIDTH), BF16),
                   jax.ShapeDtypeStruct((b, t, ATT_WIDTH), F32),
                   jax.ShapeDtypeStruct((b, t, ATT_WIDTH), F32),
                   jax.ShapeDtypeStruct((b, t, ATT_WIDTH), BF16),
                   jax.ShapeDtypeStruct((b, t, ATT_WIDTH), BF16),
                   jax.ShapeDtypeStruct((b, HIST, c), F32)],
        scratch_shapes=[pltpu.VMEM((HIST_PAD + tm, c), F32)],
        compiler_params=pltpu.CompilerParams(
            dimension_semantics=("arbitrary", "arbitrary"), vmem_limit_bytes=VMEM_LIMIT_BYTES),
        name="inproj_conv",
    )(x, w_a, past_pad, conv_w, conv_b.reshape(1, c), ln_g.reshape(1, c), ln_b.reshape(1, c))


def _prompt_attn_body(bias_ref, q_ref, k_ref, v_ref, o_ref, *, blk):
    qi = pl.program_id(1)
    tri = _tri_ge(blk)
    row = lax.broadcasted_iota(jnp.int32, (blk, blk), 0)
    col = lax.broadcasted_iota(jnp.int32, (blk, blk), 1)
    causal = col < row
    lane = lax.broadcasted_iota(jnp.int32, (blk, LANES), 1)
    dims = (((1,), (1,)), ((), ()))

    for p in range(N_HEADS // HEADS_PER_LANE_TILE):
        lanes = pl.ds(p * LANES, LANES)
        qp = q_ref[0, :, lanes]
        accs = []
        for hh in range(HEADS_PER_LANE_TILE):
            bias = bias_ref[p * HEADS_PER_LANE_TILE + hh]
            in_head = (lane >= hh * HEAD_DIM) & (lane < (hh + 1) * HEAD_DIM)
            qm = jnp.where(in_head, qp, jnp.zeros_like(qp))

            start = pl.multiple_of(qi * blk, blk)
            kd = k_ref[0, pl.ds(start, blk), lanes]
            vd = v_ref[0, pl.ds(start, blk), lanes]
            z = lax.dot_general(qm, kd, dims, preferred_element_type=F32) + bias
            l = jnp.where(causal, _neg_softplus(z), 0.0)
            cs = _rev_cumsum(l, tri)
            a = jnp.exp(jnp.where(causal, z + cs, NEG_BIG))
            acc = jnp.dot(a.astype(BF16), vd, preferred_element_type=F32)
            carry = cs[:, 0:1]

            def step(jj, state, qm=qm, bias=bias, lanes=lanes):
                carry, acc = state
                start = pl.multiple_of((qi - 1 - jj) * blk, blk)
                kj = k_ref[0, pl.ds(start, blk), lanes]
                vj = v_ref[0, pl.ds(start, blk), lanes]
                z = lax.dot_general(qm, kj, dims, preferred_element_type=F32) + bias
                cs = _rev_cumsum(_neg_softplus(z), tri) + carry
                a = jnp.exp(z + cs)
                acc = acc + jnp.dot(a.astype(BF16), vj, preferred_element_type=F32)
                return cs[:, 0:1], acc

            carry, acc = lax.fori_loop(0, qi, step, (carry, acc))
            accs.append(acc)
        o_pair = jnp.where(lane < HEAD_DIM, accs[0], accs[1])
        o_ref[0, :, lanes] = o_pair.astype(BF16)


def _prompt_attention(sb_bias, q, kb, vb, *, blk):
    b, t, w = q.shape
    nq = t // blk
    assert nq * blk == t
    body = functools.partial(_prompt_attn_body, blk=blk)
    return pl.pallas_call(
        body,
        grid=(b, nq),
        in_specs=[pl.BlockSpec(memory_space=pltpu.SMEM),
                  pl.BlockSpec((1, blk, w), lambda i, j: (i, j, 0)),
                  pl.BlockSpec((1, t, w), lambda i, j: (i, 0, 0)),
                  pl.BlockSpec((1, t, w), lambda i, j: (i, 0, 0))],
        out_specs=pl.BlockSpec((1, blk, w), lambda i, j: (i, j, 0)),
        out_shape=jax.ShapeDtypeStruct((b, t, w), BF16),
        compiler_params=pltpu.CompilerParams(
            dimension_semantics=("arbitrary", "arbitrary"), vmem_limit_bytes=VMEM_LIMIT_BYTES),
        name="prompt_attention",
    )(sb_bias, q, kb, vb)


def _sample_attn_body(pt_ref, q_ref, kn_ref, vn_ref, bias_ref, *rest, ppc, nq, page):
    k_refs = rest[:ppc]
    v_refs = rest[ppc:2 * ppc]
    o_ref = rest[2 * ppc]
    qm_ref, carry_ref, acc_ref = rest[2 * ppc + 1:]
    del pt_ref
    c = pl.program_id(1)
    nc = pl.num_programs(1)
    rows = nq * N_HEADS
    dims = (((1,), (1,)), ((), ()))
    tri = _tri_ge(page)
    lane_w = lax.broadcasted_iota(jnp.int32, (N_HEADS, ATT_WIDTH), 1)
    head_w = lax.broadcasted_iota(jnp.int32, (N_HEADS, ATT_WIDTH), 0)
    head_mask = (lane_w >= head_w * HEAD_DIM) & (lane_w < (head_w + 1) * HEAD_DIM)

    def block(kb, vb, valid):
        qm = qm_ref[...].astype(BF16)
        z = lax.dot_general(qm, kb, dims, preferred_element_type=F32) + bias_ref[...]
        l = _neg_softplus(z)
        if valid is not None:
            l = jnp.where(valid, l, 0.0)
        cs = _rev_cumsum(l, tri) + carry_ref[...]
        arg = z + cs
        if valid is not None:
            arg = jnp.where(valid, arg, NEG_BIG)
        a = jnp.exp(arg)
        acc_ref[...] += jnp.dot(a.astype(BF16), vb, preferred_element_type=F32)
        carry_ref[...] = jnp.broadcast_to(cs[:, 0:1], (rows, page))

    @pl.when(c == 0)
    def _():
        qf = q_ref[0].astype(F32)
        for i in range(nq):
            qi = jnp.broadcast_to(qf[i:i + 1, :], (N_HEADS, ATT_WIDTH))
            qm_ref[pl.ds(i * N_HEADS, N_HEADS), :] = jnp.where(head_mask, qi, 0.0)
        carry_ref[...] = jnp.zeros_like(carry_ref)
        acc_ref[...] = jnp.zeros_like(acc_ref)
        key = lax.broadcasted_iota(jnp.int32, (rows, page), 1)
        qidx = lax.broadcasted_iota(jnp.int32, (rows, page), 0) // N_HEADS
        block(kn_ref[0], vn_ref[0], key < qidx)

    for p in range(ppc):
        block(k_refs[p][0].astype(BF16), v_refs[p][0].astype(BF16), None)

    @pl.when(c == nc - 1)
    def _():
        o_ref[0] = jnp.zeros(o_ref.shape[1:], F32)
        for i in range(nq):
            oi = jnp.where(head_mask, acc_ref[pl.ds(i * N_HEADS, N_HEADS), :], 0.0)
            o_ref[0, pl.ds(i, 1), :] = jnp.sum(oi, axis=0, keepdims=True)


def _sample_attention(page_table, q, kn, vn, bias_rows, cache_k, cache_v, *, nq, ppc):
    b = q.shape[0]
    n_pages = page_table.shape[1]
    page = cache_k.shape[1]
    nchunks = n_pages // ppc
    assert nchunks * ppc == n_pages
    rows = nq * N_HEADS
    body = functools.partial(_sample_attn_body, ppc=ppc, nq=nq, page=page)

    def page_spec(p):
        return pl.BlockSpec(
            (1, page, ATT_WIDTH),
            lambda i, j, pt: (pt[i, n_pages - 1 - (j * ppc + p)], 0, 0))

    per_b = lambda r: pl.BlockSpec((1, r, ATT_WIDTH), lambda i, j, pt: (i, 0, 0))
    grid_spec = pltpu.PrefetchScalarGridSpec(
        num_scalar_prefetch=1,
        grid=(b, nchunks),
        in_specs=[per_b(q.shape[1]), per_b(page), per_b(page),
                  pl.BlockSpec((rows, page), lambda i, j, pt: (0, 0))]
                 + [page_spec(p) for p in range(ppc)] * 2,
        out_specs=per_b(q.shape[1]),
        scratch_shapes=[pltpu.VMEM((rows, ATT_WIDTH), F32),
                        pltpu.VMEM((rows, page), F32),
                        pltpu.VMEM((rows, ATT_WIDTH), F32)],
    )
    return pl.pallas_call(
        body,
        grid_spec=grid_spec,
        out_shape=jax.ShapeDtypeStruct(q.shape, F32),
        compiler_params=pltpu.CompilerParams(
            dimension_semantics=("arbitrary", "arbitrary"), vmem_limit_bytes=VMEM_LIMIT_BYTES),
        name="sample_attention",
    )(page_table, q, kn, vn, bias_rows, *([cache_k] * ppc), *([cache_v] * ppc))


def _merge_body(x_ref, cact_ref, o_ref, wg_ref, wcp_ref, wap_ref, wout_ref, g_ref, b_ref,
                x1_ref, *, alpha, d):
    x = x_ref[...]
    gates = jnp.dot(x.astype(BF16), wg_ref[...], preferred_element_type=F32)
    conv_out = jnp.dot(cact_ref[...], wcp_ref[...], preferred_element_type=F32)
    att_out = jnp.dot(o_ref[...], wap_ref[...], preferred_element_type=F32)
    m = jax.nn.sigmoid(gates[:, :d]) * conv_out + jax.nn.sigmoid(gates[:, d:]) * att_out
    mixed = jnp.dot(m.astype(BF16), wout_ref[...], preferred_element_type=F32)
    x1_ref[...] = _layer_norm(alpha * x + mixed, g_ref[...], b_ref[...])


def _merge(x, cact, o, w_g, w_cp, w_ap, w_out, ln_g, ln_b, *, tm, alpha):
    n, d = x.shape
    assert n % tm == 0
    body = functools.partial(_merge_body, alpha=alpha, d=d)
    tile = lambda w: pl.BlockSpec((tm, w), lambda i: (i, 0))
    return pl.pallas_call(
        body,
        grid=(n // tm,),
        in_specs=[tile(d), tile(cact.shape[1]), tile(o.shape[1]),
                  _const_spec(w_g.shape), _const_spec(w_cp.shape), _const_spec(w_ap.shape),
                  _const_spec(w_out.shape), _const_spec((1, d)), _const_spec((1, d))],
        out_specs=tile(d),
        out_shape=jax.ShapeDtypeStruct((n, d), F32),
        compiler_params=pltpu.CompilerParams(
            dimension_semantics=("arbitrary",), vmem_limit_bytes=VMEM_LIMIT_BYTES),
        name="merge",
    )(x, cact, o, w_g, w_cp, w_ap, w_out, ln_g.reshape(1, d), ln_b.reshape(1, d))


def _ffn_body(x1_ref, p_ref, wup_ref, wdn_ref, wpg_ref, wple_ref, g_ref, b_ref, y_ref,
              *, alpha, hidden, chunk):
    x1 = x1_ref[...]
    x1b = x1.astype(BF16)
    ple = (jax.nn.sigmoid(jnp.dot(x1b, wpg_ref[...], preferred_element_type=F32))
           * jnp.dot(p_ref[...].astype(BF16), wple_ref[...], preferred_element_type=F32))
    acc = alpha * x1 + ple
    for c0 in range(0, hidden, chunk):
        gate = jnp.dot(x1b, wup_ref[:, pl.ds(c0, chunk)], preferred_element_type=F32)
        up = jnp.dot(x1b, wup_ref[:, pl.ds(hidden + c0, chunk)], preferred_element_type=F32)
        act = (gate * jax.nn.sigmoid(gate) * up).astype(BF16)
        acc = acc + jnp.dot(act, wdn_ref[pl.ds(c0, chunk), :], preferred_element_type=F32)
    y_ref[...] = _layer_norm(acc, g_ref[...], b_ref[...])


def _ffn(x1, p, w_up, w_dn, w_pg, w_ple, ln_g, ln_b, *, tm, alpha, chunk):
    n, d = x1.shape
    hidden = w_dn.shape[0]
    assert n % tm == 0 and hidden % chunk == 0
    body = functools.partial(_ffn_body, alpha=alpha, hidden=hidden, chunk=chunk)
    tile = lambda w: pl.BlockSpec((tm, w), lambda i: (i, 0))
    return pl.pallas_call(
        body,
        grid=(n // tm,),
        in_specs=[tile(d), tile(p.shape[1]),
                  _const_spec(w_up.shape), _const_spec(w_dn.shape), _const_spec(w_pg.shape),
                  _const_spec(w_ple.shape), _const_spec((1, d)), _const_spec((1, d))],
        out_specs=tile(d),
        out_shape=jax.ShapeDtypeStruct((n, d), F32),
        compiler_params=pltpu.CompilerParams(
            dimension_semantics=("arbitrary",), vmem_limit_bytes=VMEM_LIMIT_BYTES),
        name="ffn",
    )(x1, p, w_up, w_dn, w_pg, w_ple, ln_g.reshape(1, d), ln_b.reshape(1, d))


def _pick_tile(n, candidates):
    for c in candidates:
        if n % c == 0:
            return c
    return n


def _layer(x_p, x_s, p_p, p_s, cache_k, cache_v, state_conv, page_table, w, alpha):
    (w_in, sb_bias, conv_w, conv_b, conv_ln_g, conv_ln_b, w_conv_proj, w_att_proj, w_out,
     ln1_g, ln1_b, w_ffn_up, w_ffn_down, w_ple_gate, w_ple, ln2_g, ln2_b) = w
    bp, tp, d = x_p.shape
    bs, ts, _ = x_s.shape
    c = conv_w.shape[1]
    n_mix = 2 * c + 3 * ATT_WIDTH
    w_a = w_in[:, :n_mix].astype(BF16)
    w_g = w_in[:, n_mix:].astype(BF16)
    w_cp, w_ap, w_o = (a.astype(BF16) for a in (w_conv_proj, w_att_proj, w_out))
    w_up, w_dn, w_pg, w_pl = (a.astype(BF16) for a in (w_ffn_up, w_ffn_down, w_ple_gate, w_ple))
    page = cache_k.shape[1]

    tm_p = _pick_tile(tp, (256, 128, 64, 32, 16, 8))
    cact_p, q_p, k_p, v_p, kb_p, vb_p, conv_p = _inproj_conv(
        x_p, w_a, jnp.zeros((bp, HIST_PAD, c), F32), conv_w, conv_b, conv_ln_g, conv_ln_b,
        tm=tm_p, valid_last=tm_p)
    blk = _pick_tile(tp, (256, 128))
    o_p = _prompt_attention(sb_bias, q_p, kb_p, vb_p, blk=blk)

    ts_pad = -(-ts // 8) * 8
    x_s_pad = jnp.pad(x_s, ((0, 0), (0, ts_pad - ts), (0, 0)))
    past_pad = jnp.pad(state_conv, ((0, 0), (HIST_PAD - HIST, 0), (0, 0)))
    cact_s, q_s, k_s, v_s, kb_s, vb_s, conv_s = _inproj_conv(
        x_s_pad, w_a, past_pad, conv_w, conv_b, conv_ln_g, conv_ln_b, tm=ts_pad, valid_last=ts)
    kn = jnp.pad(kb_s, ((0, 0), (0, page - ts_pad), (0, 0)))
    vn = jnp.pad(vb_s, ((0, 0), (0, page - ts_pad), (0, 0)))
    bias_rows = jnp.broadcast_to(jnp.tile(sb_bias.astype(F32), ts)[:, None], (ts * N_HEADS, page))
    ppc = _pick_tile(page_table.shape[1], (8, 4, 2, 1))
    o_s = _sample_attention(page_table, q_s, kn, vn, bias_rows, cache_k, cache_v, nq=ts, ppc=ppc)

    def tail(x, cact, o, p):
        n = x.shape[0]
        tm = _pick_tile(n, (512, 256, 128, 64, 32, 16, 8))
        x1 = _merge(x, cact, o, w_g, w_cp, w_ap, w_o, ln1_g, ln1_b, tm=tm, alpha=alpha)
        return _ffn(x1, p, w_up, w_dn, w_pg, w_pl, ln2_g, ln2_b, tm=tm, alpha=alpha, chunk=256)

    y_p = tail(x_p.reshape(bp * tp, d), cact_p.reshape(bp * tp, c),
               o_p.reshape(bp * tp, ATT_WIDTH), p_p.reshape(bp * tp, -1)).reshape(bp, tp, d)
    y_s = tail(x_s.reshape(bs * ts, d), cact_s[:, :ts].reshape(bs * ts, c),
               o_s[:, :ts].reshape(bs * ts, ATT_WIDTH).astype(BF16), p_s.reshape(bs * ts, -1)).reshape(bs, ts, d)

    heads = lambda a, b, t: a[:, :t].reshape(b, t, N_HEADS, HEAD_DIM)
    return (y_p, y_s, heads(k_p, bp, tp), heads(v_p, bp, tp), conv_p,
            heads(k_s, bs, ts), heads(v_s, bs, ts), conv_s)


def kernel(x_prompt, x_sample, p_prompt, p_sample, cache_k, cache_v, state_conv, page_table, w_in, sb_bias, conv_w, conv_b, conv_ln_g, conv_ln_b, w_conv_proj, w_att_proj, w_out, ln1_g, ln1_b, w_ffn_up, w_ffn_down, w_ple_gate, w_ple, ln2_g, ln2_b):
    depth = w_in.shape[0]
    alpha = (2.0 * depth) ** 0.25
    y_p, y_s = x_prompt, x_sample
    outs = [[] for _ in range(6)]
    for i in range(depth):
        w = (w_in[i], sb_bias[i], conv_w[i], conv_b[i], conv_ln_g[i], conv_ln_b[i],
             w_conv_proj[i], w_att_proj[i], w_out[i], ln1_g[i], ln1_b[i], w_ffn_up[i],
             w_ffn_down[i], w_ple_gate[i], w_ple[i], ln2_g[i], ln2_b[i])
        n_pool, page = cache_k.shape[1], cache_k.shape[2]
        ck = cache_k.reshape(depth * n_pool, page, ATT_WIDTH)
        cv = cache_v.reshape(depth * n_pool, page, ATT_WIDTH)
        y_p, y_s, k1, v1, c1, k2, v2, c2 = _layer(
            y_p, y_s, p_prompt[i], p_sample[i], ck, cv, state_conv[i],
            page_table + i * n_pool, w, alpha)
        for lst, a in zip(outs, (k1, v1, c1, k2, v2, c2)):
            lst.append(a)
    return (y_p, y_s) + tuple(jnp.stack(lst) for lst in outs)
```

```python
import functools

import jax
import jax.numpy as jnp
from jax import lax
from jax.experimental import pallas as pl
from jax.experimental.pallas import tpu as pltpu

F32 = jnp.float32
BF16 = jnp.bfloat16

N_HEADS = 8
HEAD_DIM = 64
ATT_WIDTH = N_HEADS * HEAD_DIM
CONV_WIDTH = 31
HIST = CONV_WIDTH - 1
HIST_PAD = 32
LN_EPS = 1e-5
LANES = 128
SUBLANES = 8
HEADS_PER_LANE_TILE = LANES // HEAD_DIM
VMEM_LIMIT_BYTES = 56 * 1024 * 1024
NEG_BIG = -1e30
LOG2E = 1.4426950408889634


def _layer_norm(x, g, b):
    mu = jnp.mean(x, axis=-1, keepdims=True)
    xc = x - mu
    var = jnp.mean(xc * xc, axis=-1, keepdims=True)
    return xc * lax.rsqrt(var + LN_EPS) * g + b


def _neg_softplus(z):
    return -(jnp.maximum(z, 0.0) + jnp.log1p(jnp.exp(-jnp.abs(z))))


def _softplus2(z2):
    return jnp.maximum(z2, 0.0) + jnp.log2(1.0 + jnp.exp2(-jnp.abs(z2)))


def _split_bf16(x):
    hi = x.astype(BF16)
    lo = (x - hi.astype(F32)).astype(BF16)
    return hi, lo


def _tri_ge(n):
    j = lax.broadcasted_iota(jnp.int32, (n, n), 0)
    s = lax.broadcasted_iota(jnp.int32, (n, n), 1)
    return (j >= s).astype(BF16)


def _rev_cumsum(l, tri):
    hi, lo = _split_bf16(l)
    return (jnp.dot(hi, tri, preferred_element_type=F32)
            + jnp.dot(lo, tri, preferred_element_type=F32))


def _const_spec(shape):
    nd = len(shape)
    return pl.BlockSpec(shape, lambda *_: (0,) * nd, pipeline_mode=pl.Buffered(1))


def _inproj_conv_body(x_ref, w_ref, past_ref, cw_ref, cb_ref, lg_ref, lb_ref,
                      cact_ref, q_ref, k_ref, v_ref, kb_ref, vb_ref, nc_ref, ubuf,
                      *, tm, nt, valid_last, conv_ch):
    t = pl.program_id(1)

    @pl.when(t == 0)
    def _():
        ubuf[pl.ds(0, HIST_PAD), :] = past_ref[0]

    xb = x_ref[0].astype(BF16)
    h = jnp.dot(xb, w_ref[...], preferred_element_type=F32)
    c = conv_ch
    u = h[:, :c] * jax.nn.sigmoid(h[:, c:2 * c])
    q = h[:, 2 * c:2 * c + ATT_WIDTH]
    k = h[:, 2 * c + ATT_WIDTH:2 * c + 2 * ATT_WIDTH]
    v = h[:, 2 * c + 2 * ATT_WIDTH:2 * c + 3 * ATT_WIDTH]
    q_ref[0] = (q * (HEAD_DIM ** -0.5)).astype(BF16)
    k_ref[0] = k
    v_ref[0] = v
    kb_ref[0] = k.astype(BF16)
    vb_ref[0] = v.astype(BF16)

    ubuf[pl.ds(HIST_PAD, tm), :] = u
    acc = jnp.zeros((tm, c), F32) + cb_ref[...]
    for rho in range(SUBLANES):
        taps = [kk for kk in range(CONV_WIDTH) if (HIST_PAD - HIST + kk) % SUBLANES == rho]
        ext = SUBLANES if rho else 0
        part = None
        for kk in taps:
            term = (ubuf[pl.ds(HIST_PAD - HIST + kk - rho, tm + ext), :]
                    * cw_ref[pl.ds(kk, 1), :])
            part = term if part is None else part + term
        acc = acc + part[rho:rho + tm, :]
    y = _layer_norm(acc, lg_ref[...], lb_ref[...])
    cact_ref[0] = (y * jax.nn.sigmoid(y)).astype(BF16)

    @pl.when(t == nt - 1)
    def _():
        nc_ref[0] = ubuf[pl.ds(HIST_PAD + valid_last - HIST, HIST), :]

    if nt > 1:
        ubuf[pl.ds(0, HIST_PAD), :] = ubuf[pl.ds(tm, HIST_PAD), :]


def _inproj_conv(x, w_a, past_pad, conv_w, conv_b, ln_g, ln_b, *, tm, valid_last):
    b, t, d = x.shape
    c = conv_w.shape[1]
    nt = t // tm
    assert nt * tm == t and tm % 8 == 0
    body = functools.partial(_inproj_conv_body, tm=tm, nt=nt, valid_last=valid_last, conv_ch=c)
    tile = lambda w: pl.BlockSpec((1, tm, w), lambda i, j: (i, j, 0))
    per_b = lambda r, w: pl.BlockSpec((1, r, w), lambda i, j: (i, 0, 0))
    return pl.pallas_call(
        body,
        grid=(b, nt),
        in_specs=[tile(d), _const_spec(w_a.shape), per_b(HIST_PAD, c),
                  _const_spec(conv_w.shape), _const_spec((1, c)), _const_spec((1, c)),
                  _const_spec((1, c))],
        out_specs=[tile(c), tile(ATT_WIDTH), tile(ATT_WIDTH), tile(ATT_WIDTH), tile(ATT_WIDTH),
                   tile(ATT_WIDTH), per_b(HIST, c)],
        out_shape=[jax.ShapeDtypeStruct((b, t, c), BF16),
                   jax.ShapeDtypeStruct((b, t, ATT_WIDTH), BF16),
                   jax.ShapeDtypeStruct((b, t, ATT_WIDTH), F32),
                   jax.ShapeDtypeStruct((b, t, ATT_WIDTH), F32),
                   jax.ShapeDtypeStruct((b, t, ATT_WIDTH), BF16),
                   jax.ShapeDtypeStruct((b, t, ATT_WIDTH), BF16),
                   jax.ShapeDtypeStruct((b, HIST, c), F32)],
        scratch_shapes=[pltpu.VMEM((HIST_PAD + tm, c), F32)],
        compiler_params=pltpu.CompilerParams(
            dimension_semantics=("arbitrary", "arbitrary"), vmem_limit_bytes=VMEM_LIMIT_BYTES),
        name="inproj_conv",
    )(x, w_a, past_pad, conv_w, conv_b.reshape(1, c), ln_g.reshape(1, c), ln_b.reshape(1, c))


def _prompt_attn_body(bias_ref, q_ref, k_ref, v_ref, o_ref, qm_ref, tri_ref, acc_ref, *, blk):
    qi = pl.program_id(1)
    row = lax.broadcasted_iota(jnp.int32, (blk, blk), 0)
    col = lax.broadcasted_iota(jnp.int32, (blk, blk), 1)
    lane = lax.broadcasted_iota(jnp.int32, (blk, LANES), 1)
    dims = (((1,), (1,)), ((), ()))
    tri = _tri_ge(blk)
    tri_ref[pl.ds(0, blk), :] = tri
    tri_ref[pl.ds(blk, blk), :] = tri
    for h in range(N_HEADS):
        p, hh = divmod(h, HEADS_PER_LANE_TILE)
        qp = q_ref[0, :, pl.ds(p * LANES, LANES)]
        in_head = (lane >= hh * HEAD_DIM) & (lane < (hh + 1) * HEAD_DIM)
        qm_ref[h] = jnp.where(in_head, qp, jnp.zeros_like(qp))

    heads = range(N_HEADS)

    def key_block(start, carries, causal):
        lanes = [pl.ds((h // HEADS_PER_LANE_TILE) * LANES, LANES) for h in heads]
        z2 = [(lax.dot_general(qm_ref[h], k_ref[0, pl.ds(start, blk), lanes[h]], dims,
                               preferred_element_type=F32) + bias_ref[h]) * LOG2E for h in heads]
        lhs = []
        for h in heads:
            sp = _softplus2(z2[h])
            if causal is not None:
                sp = jnp.where(causal, sp, 0.0)
            lhs.append(jnp.concatenate(_split_bf16(sp), axis=1))
        cs = [jnp.dot(lhs[h], tri_ref[...], preferred_element_type=F32) for h in heads]
        if carries is not None:
            cs = [cs[h] + carries[h] for h in heads]
        a = []
        for h in heads:
            arg = z2[h] - cs[h]
            if causal is not None:
                arg = jnp.where(causal, arg, NEG_BIG)
            a.append(jnp.exp2(arg).astype(BF16))
        for h in heads:
            pv = jnp.dot(a[h], v_ref[0, pl.ds(start, blk), lanes[h]], preferred_element_type=F32)
            if carries is None:
                acc_ref[h] = pv
            else:
                acc_ref[h] += pv
        return tuple(c[:, 0:1] for c in cs)

    carries = key_block(pl.multiple_of(qi * blk, blk), None, col < row)
    lax.fori_loop(
        0, qi, lambda jj, c: key_block(pl.multiple_of((qi - 1 - jj) * blk, blk), c, None), carries)
    for p in range(N_HEADS // HEADS_PER_LANE_TILE):
        o_pair = jnp.where(lane < HEAD_DIM, acc_ref[HEADS_PER_LANE_TILE * p],
                           acc_ref[HEADS_PER_LANE_TILE * p + 1])
        o_ref[0, :, pl.ds(p * LANES, LANES)] = o_pair.astype(BF16)


def _prompt_attention(sb_bias, q, kb, vb, *, blk):
    b, t, w = q.shape
    nq = t // blk
    assert nq * blk == t and HEADS_PER_LANE_TILE == 2
    body = functools.partial(_prompt_attn_body, blk=blk)
    return pl.pallas_call(
        body,
        grid=(b, nq),
        in_specs=[pl.BlockSpec(memory_space=pltpu.SMEM),
                  pl.BlockSpec((1, blk, w), lambda i, j: (i, j, 0)),
                  pl.BlockSpec((1, t, w), lambda i, j: (i, 0, 0)),
                  pl.BlockSpec((1, t, w), lambda i, j: (i, 0, 0))],
        out_specs=pl.BlockSpec((1, blk, w), lambda i, j: (i, j, 0)),
        out_shape=jax.ShapeDtypeStruct((b, t, w), BF16),
        scratch_shapes=[pltpu.VMEM((N_HEADS, blk, LANES), BF16),
                        pltpu.VMEM((2 * blk, blk), BF16),
                        pltpu.VMEM((N_HEADS, blk, LANES), F32)],
        compiler_params=pltpu.CompilerParams(
            dimension_semantics=("arbitrary", "arbitrary"), vmem_limit_bytes=VMEM_LIMIT_BYTES),
        name="prompt_attention",
    )(sb_bias, q, kb, vb)


def _sample_attn_body(pt_ref, q_ref, knt_ref, vnt_ref, bias_ref, *rest, ppc, page):
    k_refs = rest[:ppc]
    v_refs = rest[ppc:2 * ppc]
    o_ref = rest[2 * ppc]
    qh_ref, tri_ref, carry_ref, acc_ref = rest[2 * ppc + 1:]
    del pt_ref
    c = pl.program_id(1)
    nc = pl.num_programs(1)
    rows = q_ref.shape[1]
    dims_nt = (((1,), (1,)), ((), ()))
    heads = range(N_HEADS)

    def process(kt_heads, vt_heads, n, valid):
        z2 = [(jnp.dot(qh_ref[h].astype(BF16), kt_heads[h], preferred_element_type=F32)
               + bias_ref[h]) * LOG2E for h in heads]
        zs = jnp.concatenate([z2[h][:, p * page:(p + 1) * page] for h in heads for p in range(n)],
                             axis=0)
        sp = _softplus2(zs)
        if valid is not None:
            sp = jnp.where(valid, sp, 0.0)
        r = jnp.dot(jnp.concatenate(_split_bf16(sp), axis=1), tri_ref[...],
                    preferred_element_type=F32)
        local, total = r[:, :page], r[:, page:]
        a = []
        for h in heads:
            carry = carry_ref[h]
            tiles = []
            for p in range(n):
                rs = slice((h * n + p) * rows, (h * n + p + 1) * rows)
                arg = zs[rs] - (local[rs] + carry)
                if valid is not None:
                    arg = jnp.where(valid[rs], arg, NEG_BIG)
                tiles.append(jnp.exp2(arg))
                carry = carry + total[rs]
            carry_ref[h] = carry
            a.append(jnp.concatenate(tiles, axis=1).astype(BF16))
        for h in heads:
            acc_ref[h] += lax.dot_general(a[h], vt_heads[h], dims_nt, preferred_element_type=F32)

    @pl.when(c == 0)
    def _():
        tri = _tri_ge(page)
        both = jnp.concatenate([tri, jnp.ones_like(tri)], axis=1)
        tri_ref[pl.ds(0, page), :] = both
        tri_ref[pl.ds(page, page), :] = both
        for h in heads:
            qh_ref[h] = q_ref[0, :, pl.ds(h * HEAD_DIM, HEAD_DIM)].astype(F32)
        carry_ref[...] = jnp.zeros_like(carry_ref)
        acc_ref[...] = jnp.zeros_like(acc_ref)
        key = lax.broadcasted_iota(jnp.int32, (N_HEADS * rows, page), 1)
        qidx = lax.broadcasted_iota(jnp.int32, (N_HEADS * rows, page), 0) % rows
        process([knt_ref[0, pl.ds(h * HEAD_DIM, HEAD_DIM), :] for h in heads],
                [vnt_ref[0, pl.ds(h * HEAD_DIM, HEAD_DIM), :] for h in heads], 1, key < qidx)

    def head_slab(refs, h):
        return jnp.concatenate([r[h] for r in refs], axis=1).astype(BF16)

    process([head_slab(k_refs, h) for h in heads], [head_slab(v_refs, h) for h in heads], ppc, None)

    @pl.when(c == nc - 1)
    def _():
        for h in heads:
            o_ref[0, :, pl.ds(h * HEAD_DIM, HEAD_DIM)] = acc_ref[h]


def _sample_attention(page_table, q, knt, vnt, sb_bias, cache_kt, cache_vt, layer, *, ppc):
    b, rows, _ = q.shape
    n_pages = page_table.shape[1]
    page = cache_kt.shape[4]
    nchunks = n_pages // ppc
    assert nchunks * ppc == n_pages and rows % 8 == 0 and page == LANES
    body = functools.partial(_sample_attn_body, ppc=ppc, page=page)

    def page_spec(p):
        return pl.BlockSpec(
            (None, None, N_HEADS, HEAD_DIM, page),
            lambda i, j, pt: (layer, pt[i, n_pages - 1 - (j * ppc + p)], 0, 0, 0))

    per_b = lambda r, w: pl.BlockSpec((1, r, w), lambda i, j, pt: (i, 0, 0))
    grid_spec = pltpu.PrefetchScalarGridSpec(
        num_scalar_prefetch=1,
        grid=(b, nchunks),
        in_specs=[per_b(rows, ATT_WIDTH), per_b(ATT_WIDTH, page), per_b(ATT_WIDTH, page),
                  pl.BlockSpec(memory_space=pltpu.SMEM)]
                 + [page_spec(p) for p in range(ppc)] * 2,
        out_specs=per_b(rows, ATT_WIDTH),
        scratch_shapes=[pltpu.VMEM((N_HEADS, rows, HEAD_DIM), F32),
                        pltpu.VMEM((2 * page, 2 * page), BF16),
                        pltpu.VMEM((N_HEADS, rows, page), F32),
                        pltpu.VMEM((N_HEADS, rows, HEAD_DIM), F32)],
    )
    return pl.pallas_call(
        body,
        grid_spec=grid_spec,
        out_shape=jax.ShapeDtypeStruct(q.shape, F32),
        compiler_params=pltpu.CompilerParams(
            dimension_semantics=("arbitrary", "arbitrary"), vmem_limit_bytes=VMEM_LIMIT_BYTES),
        name="sample_attention",
    )(page_table, q, knt, vnt, sb_bias, *([cache_kt] * ppc), *([cache_vt] * ppc))


def _merge_body(x_ref, cact_ref, o_ref, wg_ref, wcp_ref, wap_ref, wout_ref, g_ref, b_ref,
                x1_ref, *, alpha, d):
    x = x_ref[...]
    gates = jnp.dot(x.astype(BF16), wg_ref[...], preferred_element_type=F32)
    conv_out = jnp.dot(cact_ref[...], wcp_ref[...], preferred_element_type=F32)
    att_out = jnp.dot(o_ref[...], wap_ref[...], preferred_element_type=F32)
    m = jax.nn.sigmoid(gates[:, :d]) * conv_out + jax.nn.sigmoid(gates[:, d:]) * att_out
    mixed = jnp.dot(m.astype(BF16), wout_ref[...], preferred_element_type=F32)
    x1_ref[...] = _layer_norm(alpha * x + mixed, g_ref[...], b_ref[...])


def _merge(x, cact, o, w_g, w_cp, w_ap, w_out, ln_g, ln_b, *, tm, alpha):
    n, d = x.shape
    assert n % tm == 0
    body = functools.partial(_merge_body, alpha=alpha, d=d)
    tile = lambda w: pl.BlockSpec((tm, w), lambda i: (i, 0))
    return pl.pallas_call(
        body,
        grid=(n // tm,),
        in_specs=[tile(d), tile(cact.shape[1]), tile(o.shape[1]),
                  _const_spec(w_g.shape), _const_spec(w_cp.shape), _const_spec(w_ap.shape),
                  _const_spec(w_out.shape), _const_spec((1, d)), _const_spec((1, d))],
        out_specs=tile(d),
        out_shape=jax.ShapeDtypeStruct((n, d), F32),
        compiler_params=pltpu.CompilerParams(
            dimension_semantics=("arbitrary",), vmem_limit_bytes=VMEM_LIMIT_BYTES),
        name="merge",
    )(x, cact, o, w_g, w_cp, w_ap, w_out, ln_g.reshape(1, d), ln_b.reshape(1, d))


def _ffn_body(x1_ref, p_ref, wup_ref, wdn_ref, wpg_ref, wple_ref, g_ref, b_ref, y_ref,
              *, alpha, hidden, chunk):
    x1 = x1_ref[...]
    x1b = x1.astype(BF16)
    ple = (jax.nn.sigmoid(jnp.dot(x1b, wpg_ref[...], preferred_element_type=F32))
           * jnp.dot(p_ref[...].astype(BF16), wple_ref[...], preferred_element_type=F32))
    acc = alpha * x1 + ple
    for c0 in range(0, hidden, chunk):
        gate = jnp.dot(x1b, wup_ref[:, pl.ds(c0, chunk)], preferred_element_type=F32)
        up = jnp.dot(x1b, wup_ref[:, pl.ds(hidden + c0, chunk)], preferred_element_type=F32)
        act = (gate * jax.nn.sigmoid(gate) * up).astype(BF16)
        acc = acc + jnp.dot(act, wdn_ref[pl.ds(c0, chunk), :], preferred_element_type=F32)
    y_ref[...] = _layer_norm(acc, g_ref[...], b_ref[...])


def _ffn(x1, p, w_up, w_dn, w_pg, w_ple, ln_g, ln_b, *, tm, alpha, chunk):
    n, d = x1.shape
    hidden = w_dn.shape[0]
    assert n % tm == 0 and hidden % chunk == 0
    body = functools.partial(_ffn_body, alpha=alpha, hidden=hidden, chunk=chunk)
    tile = lambda w: pl.BlockSpec((tm, w), lambda i: (i, 0))
    return pl.pallas_call(
        body,
        grid=(n // tm,),
        in_specs=[tile(d), tile(p.shape[1]),
                  _const_spec(w_up.shape), _const_spec(w_dn.shape), _const_spec(w_pg.shape),
                  _const_spec(w_ple.shape), _const_spec((1, d)), _const_spec((1, d))],
        out_specs=tile(d),
        out_shape=jax.ShapeDtypeStruct((n, d), F32),
        compiler_params=pltpu.CompilerParams(
            dimension_semantics=("arbitrary",), vmem_limit_bytes=VMEM_LIMIT_BYTES),
        name="ffn",
    )(x1, p, w_up, w_dn, w_pg, w_ple, ln_g.reshape(1, d), ln_b.reshape(1, d))


def _pick_tile(n, candidates):
    for c in candidates:
        if n % c == 0:
            return c
    return n


def _layer(x_p, x_s, p_p, p_s, cache_kt, cache_vt, layer, state_conv, page_table, w, alpha):
    (w_in, sb_bias, conv_w, conv_b, conv_ln_g, conv_ln_b, w_conv_proj, w_att_proj, w_out,
     ln1_g, ln1_b, w_ffn_up, w_ffn_down, w_ple_gate, w_ple, ln2_g, ln2_b) = w
    bp, tp, d = x_p.shape
    bs, ts, _ = x_s.shape
    c = conv_w.shape[1]
    n_mix = 2 * c + 3 * ATT_WIDTH
    w_a = w_in[:, :n_mix].astype(BF16)
    w_g = w_in[:, n_mix:].astype(BF16)
    w_cp, w_ap, w_o = (a.astype(BF16) for a in (w_conv_proj, w_att_proj, w_out))
    w_up, w_dn, w_pg, w_pl = (a.astype(BF16) for a in (w_ffn_up, w_ffn_down, w_ple_gate, w_ple))
    page = cache_kt.shape[4]

    tm_p = _pick_tile(tp, (256, 128, 64, 32, 16, 8))
    cact_p, q_p, k_p, v_p, kb_p, vb_p, conv_p = _inproj_conv(
        x_p, w_a, jnp.zeros((bp, HIST_PAD, c), F32), conv_w, conv_b, conv_ln_g, conv_ln_b,
        tm=tm_p, valid_last=tm_p)
    blk = _pick_tile(tp, (256, 128))
    o_p = _prompt_attention(sb_bias, q_p, kb_p, vb_p, blk=blk)

    ts_pad = -(-ts // 8) * 8
    x_s_pad = jnp.pad(x_s, ((0, 0), (0, ts_pad - ts), (0, 0)))
    past_pad = jnp.pad(state_conv, ((0, 0), (HIST_PAD - HIST, 0), (0, 0)))
    cact_s, q_s, k_s, v_s, kb_s, vb_s, conv_s = _inproj_conv(
        x_s_pad, w_a, past_pad, conv_w, conv_b, conv_ln_g, conv_ln_b, tm=ts_pad, valid_last=ts)
    knt = jnp.pad(jnp.swapaxes(kb_s, 1, 2), ((0, 0), (0, 0), (0, page - ts_pad)))
    vnt = jnp.pad(jnp.swapaxes(vb_s, 1, 2), ((0, 0), (0, 0), (0, page - ts_pad)))
    ppc = _pick_tile(page_table.shape[1], (16, 8, 4, 2, 1))
    o_s = _sample_attention(page_table, q_s, knt, vnt, sb_bias, cache_kt, cache_vt, layer, ppc=ppc)

    def tail(x, cact, o, p):
        n = x.shape[0]
        tm = _pick_tile(n, (512, 256, 128, 64, 32, 16, 8))
        x1 = _merge(x, cact, o, w_g, w_cp, w_ap, w_o, ln1_g, ln1_b, tm=tm, alpha=alpha)
        return _ffn(x1, p, w_up, w_dn, w_pg, w_pl, ln2_g, ln2_b, tm=tm, alpha=alpha, chunk=256)

    y_p = tail(x_p.reshape(bp * tp, d), cact_p.reshape(bp * tp, c),
               o_p.reshape(bp * tp, ATT_WIDTH), p_p.reshape(bp * tp, -1)).reshape(bp, tp, d)
    y_s = tail(x_s.reshape(bs * ts, d), cact_s[:, :ts].reshape(bs * ts, c),
               o_s[:, :ts].reshape(bs * ts, ATT_WIDTH).astype(BF16), p_s.reshape(bs * ts, -1)).reshape(bs, ts, d)

    heads = lambda a, b, t: a[:, :t].reshape(b, t, N_HEADS, HEAD_DIM)
    return (y_p, y_s, heads(k_p, bp, tp), heads(v_p, bp, tp), conv_p,
            heads(k_s, bs, ts), heads(v_s, bs, ts), conv_s)


def kernel(x_prompt, x_sample, p_prompt, p_sample, cache_k, cache_v, state_conv, page_table, w_in, sb_bias, conv_w, conv_b, conv_ln_g, conv_ln_b, w_conv_proj, w_att_proj, w_out, ln1_g, ln1_b, w_ffn_up, w_ffn_down, w_ple_gate, w_ple, ln2_g, ln2_b):
    depth = w_in.shape[0]
    alpha = (2.0 * depth) ** 0.25
    y_p, y_s = x_prompt, x_sample
    outs = [[] for _ in range(6)]
    cache_kt = jnp.transpose(cache_k, (0, 1, 3, 4, 2))
    cache_vt = jnp.transpose(cache_v, (0, 1, 3, 4, 2))
    for i in range(depth):
        w = (w_in[i], sb_bias[i], conv_w[i], conv_b[i], conv_ln_g[i], conv_ln_b[i],
             w_conv_proj[i], w_att_proj[i], w_out[i], ln1_g[i], ln1_b[i], w_ffn_up[i],
             w_ffn_down[i], w_ple_gate[i], w_ple[i], ln2_g[i], ln2_b[i])
        y_p, y_s, k1, v1, c1, k2, v2, c2 = _layer(
            y_p, y_s, p_prompt[i], p_sample[i], cache_kt, cache_vt, i, state_conv[i],
            page_table, w, alpha)
        for lst, a in zip(outs, (k1, v1, c1, k2, v2, c2)):
            lst.append(a)
    return (y_p, y_s) + tuple(jnp.stack(lst) for lst in outs)
```

```python
import functools

import jax
import jax.numpy as jnp
from jax import lax
from jax.experimental import pallas as pl
from jax.experimental.pallas import tpu as pltpu

F32 = jnp.float32
BF16 = jnp.bfloat16

N_HEADS = 8
HEAD_DIM = 64
ATT_WIDTH = N_HEADS * HEAD_DIM
CONV_WIDTH = 31
HIST = CONV_WIDTH - 1
HIST_PAD = 32
LN_EPS = 1e-5
LANES = 128
SUBLANES = 8
HEADS_PER_LANE_TILE = LANES // HEAD_DIM
VMEM_LIMIT_BYTES = 56 * 1024 * 1024
NEG_BIG = -1e30
LOG2E = 1.4426950408889634


def _layer_norm(x, g, b):
    mu = jnp.mean(x, axis=-1, keepdims=True)
    xc = x - mu
    var = jnp.mean(xc * xc, axis=-1, keepdims=True)
    return xc * lax.rsqrt(var + LN_EPS) * g + b


def _softplus2(z2):
    return jnp.maximum(z2, 0.0) + jnp.log2(1.0 + jnp.exp2(-jnp.abs(z2)))


def _split_bf16(x):
    hi = x.astype(BF16)
    lo = (x - hi.astype(F32)).astype(BF16)
    return hi, lo


def _tri_ge(n):
    j = lax.broadcasted_iota(jnp.int32, (n, n), 0)
    s = lax.broadcasted_iota(jnp.int32, (n, n), 1)
    return (j >= s).astype(BF16)


def _const_spec(shape):
    nd = len(shape)
    return pl.BlockSpec(shape, lambda *_: (0,) * nd, pipeline_mode=pl.Buffered(1))


def _conv_taps(ub, cw_ref, cb_ref, tm):
    acc = jnp.zeros((tm, ub.shape[1]), F32) + cb_ref[...]
    for rho in range(SUBLANES):
        taps = [kk for kk in range(CONV_WIDTH) if (HIST_PAD - HIST + kk) % SUBLANES == rho]
        ext = SUBLANES if rho else 0
        part = None
        for kk in taps:
            term = ub[pl.ds(HIST_PAD - HIST + kk - rho, tm + ext), :] * cw_ref[pl.ds(kk, 1), :]
            part = term if part is None else part + term
        acc = acc + part[rho:rho + tm, :]
    return acc


def _inproj_conv_body(x_ref, w_ref, past_ref, cw_ref, cb_ref, lg_ref, lb_ref,
                      cact_ref, q_ref, k_ref, v_ref, kb_ref, vb_ref, nc_ref, ubuf,
                      *, bb, tm, nt, valid_last, conv_ch):
    t = pl.program_id(1)
    d = x_ref.shape[2]

    @pl.when(t == 0)
    def _():
        ubuf[:, pl.ds(0, HIST_PAD), :] = past_ref[...]

    xb = x_ref[...].reshape(bb * tm, d).astype(BF16)
    h = jnp.dot(xb, w_ref[...], preferred_element_type=F32)
    c = conv_ch
    u = h[:, :c] * jax.nn.sigmoid(h[:, c:2 * c])
    q = h[:, 2 * c:2 * c + ATT_WIDTH]
    k = h[:, 2 * c + ATT_WIDTH:2 * c + 2 * ATT_WIDTH]
    v = h[:, 2 * c + 2 * ATT_WIDTH:2 * c + 3 * ATT_WIDTH]
    per_seq = lambda a: a.reshape(bb, tm, a.shape[1])
    q_ref[...] = per_seq((q * (HEAD_DIM ** -0.5)).astype(BF16))
    k_ref[...] = per_seq(k)
    v_ref[...] = per_seq(v)
    kb_ref[...] = per_seq(k.astype(BF16))
    vb_ref[...] = per_seq(v.astype(BF16))

    ubuf[:, pl.ds(HIST_PAD, tm), :] = per_seq(u)
    acc = [_conv_taps(ubuf.at[s], cw_ref, cb_ref, tm) for s in range(bb)]
    acc = acc[0] if bb == 1 else jnp.concatenate(acc, axis=0)
    y = _layer_norm(acc, lg_ref[...], lb_ref[...])
    cact_ref[...] = per_seq((y * jax.nn.sigmoid(y)).astype(BF16))

    @pl.when(t == nt - 1)
    def _():
        nc_ref[...] = ubuf[:, pl.ds(HIST_PAD + valid_last - HIST, HIST), :]

    if nt > 1:
        ubuf[:, pl.ds(0, HIST_PAD), :] = ubuf[:, pl.ds(tm, HIST_PAD), :]


def _inproj_conv(x, w_a, past_pad, conv_w, conv_b, ln_g, ln_b, *, bb, tm, valid_last):
    b, t, d = x.shape
    c = conv_w.shape[1]
    nt = t // tm
    assert nt * tm == t and tm % SUBLANES == 0 and b % bb == 0
    body = functools.partial(_inproj_conv_body, bb=bb, tm=tm, nt=nt, valid_last=valid_last,
                             conv_ch=c)
    tile = lambda w: pl.BlockSpec((bb, tm, w), lambda i, j: (i, j, 0))
    per_b = lambda r, w: pl.BlockSpec((bb, r, w), lambda i, j: (i, 0, 0))
    return pl.pallas_call(
        body,
        grid=(b // bb, nt),
        in_specs=[tile(d), _const_spec(w_a.shape), per_b(HIST_PAD, c),
                  _const_spec(conv_w.shape), _const_spec((1, c)), _const_spec((1, c)),
                  _const_spec((1, c))],
        out_specs=[tile(c), tile(ATT_WIDTH), tile(ATT_WIDTH), tile(ATT_WIDTH), tile(ATT_WIDTH),
                   tile(ATT_WIDTH), per_b(HIST, c)],
        out_shape=[jax.ShapeDtypeStruct((b, t, c), BF16),
                   jax.ShapeDtypeStruct((b, t, ATT_WIDTH), BF16),
                   jax.ShapeDtypeStruct((b, t, ATT_WIDTH), F32),
                   jax.ShapeDtypeStruct((b, t, ATT_WIDTH), F32),
                   jax.ShapeDtypeStruct((b, t, ATT_WIDTH), BF16),
                   jax.ShapeDtypeStruct((b, t, ATT_WIDTH), BF16),
                   jax.ShapeDtypeStruct((b, HIST, c), F32)],
        scratch_shapes=[pltpu.VMEM((bb, HIST_PAD + tm, c), F32)],
        compiler_params=pltpu.CompilerParams(
            dimension_semantics=("arbitrary", "arbitrary"), vmem_limit_bytes=VMEM_LIMIT_BYTES),
        name="inproj_conv",
    )(x, w_a, past_pad, conv_w, conv_b.reshape(1, c), ln_g.reshape(1, c), ln_b.reshape(1, c))


def _prompt_attn_body(bias_ref, q_ref, k_ref, v_ref, o_ref, qm_ref, tri_ref, acc_ref, *, blk):
    qi = pl.program_id(1)
    row = lax.broadcasted_iota(jnp.int32, (blk, blk), 0)
    col = lax.broadcasted_iota(jnp.int32, (blk, blk), 1)
    lane = lax.broadcasted_iota(jnp.int32, (blk, LANES), 1)
    dims = (((1,), (1,)), ((), ()))
    tri = _tri_ge(blk)
    tri_ref[pl.ds(0, blk), :] = tri
    tri_ref[pl.ds(blk, blk), :] = tri
    for h in range(N_HEADS):
        p, hh = divmod(h, HEADS_PER_LANE_TILE)
        qp = q_ref[0, :, pl.ds(p * LANES, LANES)]
        in_head = (lane >= hh * HEAD_DIM) & (lane < (hh + 1) * HEAD_DIM)
        qm_ref[h] = jnp.where(in_head, qp, jnp.zeros_like(qp))

    heads = range(N_HEADS)

    def key_blocks(lo_block, nb, carries, causal):
        keys = pl.ds(pl.multiple_of(lo_block * blk, blk), nb * blk)
        lanes = [pl.ds((h // HEADS_PER_LANE_TILE) * LANES, LANES) for h in heads]
        z2 = [(lax.dot_general(qm_ref[h], k_ref[0, keys, lanes[h]], dims,
                               preferred_element_type=F32) + bias_ref[h]) * LOG2E for h in heads]
        z2 = [[z[:, i * blk:(i + 1) * blk] for i in range(nb)] for z in z2]
        lhs = []
        for h in heads:
            for i in range(nb):
                sp = _softplus2(z2[h][i])
                if causal is not None and i == nb - 1:
                    sp = jnp.where(causal, sp, 0.0)
                lhs.append(jnp.concatenate(_split_bf16(sp), axis=1))
        local = [jnp.dot(x, tri_ref[...], preferred_element_type=F32) for x in lhs]
        a, new_carries = [], []
        for h in heads:
            carry = None if carries is None else carries[h]
            tiles = [None] * nb
            for i in reversed(range(nb)):
                cs = local[h * nb + i] if carry is None else local[h * nb + i] + carry
                carry = cs[:, 0:1]
                arg = z2[h][i] - cs
                if causal is not None and i == nb - 1:
                    arg = jnp.where(causal, arg, NEG_BIG)
                tiles[i] = jnp.exp2(arg).astype(BF16)
            a.append(tiles[0] if nb == 1 else jnp.concatenate(tiles, axis=1))
            new_carries.append(carry)
        for h in heads:
            pv = jnp.dot(a[h], v_ref[0, keys, lanes[h]], preferred_element_type=F32)
            if carries is None:
                acc_ref[h] = pv
            else:
                acc_ref[h] += pv
        return tuple(new_carries)

    odd = qi % 2
    carries = lax.cond(odd == 0,
                       lambda: key_blocks(qi, 1, None, col < row),
                       lambda: key_blocks(qi - 1, 2, None, col < row))
    top = qi - 1 - odd
    lax.fori_loop(0, qi // 2, lambda jj, c: key_blocks(top - 1 - 2 * jj, 2, c, None), carries)
    for p in range(N_HEADS // HEADS_PER_LANE_TILE):
        o_pair = jnp.where(lane < HEAD_DIM, acc_ref[HEADS_PER_LANE_TILE * p],
                           acc_ref[HEADS_PER_LANE_TILE * p + 1])
        o_ref[0, :, pl.ds(p * LANES, LANES)] = o_pair.astype(BF16)


def _prompt_attention(sb_bias, q, kb, vb, *, blk):
    b, t, w = q.shape
    nq = t // blk
    assert nq * blk == t and HEADS_PER_LANE_TILE == 2
    body = functools.partial(_prompt_attn_body, blk=blk)
    return pl.pallas_call(
        body,
        grid=(b, nq),
        in_specs=[pl.BlockSpec(memory_space=pltpu.SMEM),
                  pl.BlockSpec((1, blk, w), lambda i, j: (i, j, 0)),
                  pl.BlockSpec((1, t, w), lambda i, j: (i, 0, 0)),
                  pl.BlockSpec((1, t, w), lambda i, j: (i, 0, 0))],
        out_specs=pl.BlockSpec((1, blk, w), lambda i, j: (i, j, 0)),
        out_shape=jax.ShapeDtypeStruct((b, t, w), BF16),
        scratch_shapes=[pltpu.VMEM((N_HEADS, blk, LANES), BF16),
                        pltpu.VMEM((2 * blk, blk), BF16),
                        pltpu.VMEM((N_HEADS, blk, LANES), F32)],
        compiler_params=pltpu.CompilerParams(
            dimension_semantics=("arbitrary", "arbitrary"), vmem_limit_bytes=VMEM_LIMIT_BYTES),
        name="prompt_attention",
    )(sb_bias, q, kb, vb)


def _sample_attn_body(pt_ref, q_ref, knt_ref, vnt_ref, bias_ref, *rest, ppc, page):
    k_refs = rest[:ppc]
    v_refs = rest[ppc:2 * ppc]
    o_ref = rest[2 * ppc]
    qh_ref, tri_ref, carry_ref, acc_ref = rest[2 * ppc + 1:]
    del pt_ref
    c = pl.program_id(1)
    nc = pl.num_programs(1)
    rows = q_ref.shape[1]
    dims_nt = (((1,), (1,)), ((), ()))
    heads = range(N_HEADS)

    def process(kt_heads, vt_heads, n, valid):
        z2 = [(jnp.dot(qh_ref[h].astype(BF16), kt_heads[h], preferred_element_type=F32)
               + bias_ref[h]) * LOG2E for h in heads]
        zs = jnp.concatenate([z2[h][:, p * page:(p + 1) * page] for h in heads for p in range(n)],
                             axis=0)
        sp = _softplus2(zs)
        if valid is not None:
            sp = jnp.where(valid, sp, 0.0)
        r = jnp.dot(jnp.concatenate(_split_bf16(sp), axis=1), tri_ref[...],
                    preferred_element_type=F32)
        local, total = r[:, :page], r[:, page:]
        a = []
        for h in heads:
            carry = carry_ref[h]
            tiles = []
            for p in range(n):
                rs = slice((h * n + p) * rows, (h * n + p + 1) * rows)
                arg = zs[rs] - (local[rs] + carry)
                if valid is not None:
                    arg = jnp.where(valid[rs], arg, NEG_BIG)
                tiles.append(jnp.exp2(arg))
                carry = carry + total[rs]
            carry_ref[h] = carry
            a.append(jnp.concatenate(tiles, axis=1).astype(BF16))
        for h in heads:
            acc_ref[h] += lax.dot_general(a[h], vt_heads[h], dims_nt, preferred_element_type=F32)

    @pl.when(c == 0)
    def _():
        tri = _tri_ge(page)
        both = jnp.concatenate([tri, jnp.ones_like(tri)], axis=1)
        tri_ref[pl.ds(0, page), :] = both
        tri_ref[pl.ds(page, page), :] = both
        for h in heads:
            qh_ref[h] = q_ref[0, :, pl.ds(h * HEAD_DIM, HEAD_DIM)].astype(F32)
        carry_ref[...] = jnp.zeros_like(carry_ref)
        acc_ref[...] = jnp.zeros_like(acc_ref)
        key = lax.broadcasted_iota(jnp.int32, (N_HEADS * rows, page), 1)
        qidx = lax.broadcasted_iota(jnp.int32, (N_HEADS * rows, page), 0) % rows
        process([knt_ref[0, pl.ds(h * HEAD_DIM, HEAD_DIM), :] for h in heads],
                [vnt_ref[0, pl.ds(h * HEAD_DIM, HEAD_DIM), :] for h in heads], 1, key < qidx)

    def head_slab(refs, h):
        return jnp.concatenate([r[h] for r in refs], axis=1).astype(BF16)

    process([head_slab(k_refs, h) for h in heads], [head_slab(v_refs, h) for h in heads], ppc, None)

    @pl.when(c == nc - 1)
    def _():
        for h in heads:
            o_ref[0, :, pl.ds(h * HEAD_DIM, HEAD_DIM)] = acc_ref[h]


def _sample_attention(page_table, q, knt, vnt, sb_bias, cache_kt, cache_vt, layer, *, ppc):
    b, rows, _ = q.shape
    n_pages = page_table.shape[1]
    page = cache_kt.shape[4]
    nchunks = n_pages // ppc
    assert nchunks * ppc == n_pages and rows % 8 == 0 and page == LANES
    body = functools.partial(_sample_attn_body, ppc=ppc, page=page)

    def page_spec(p):
        return pl.BlockSpec(
            (None, None, N_HEADS, HEAD_DIM, page),
            lambda i, j, pt: (layer, pt[i, n_pages - 1 - (j * ppc + p)], 0, 0, 0))

    per_b = lambda r, w: pl.BlockSpec((1, r, w), lambda i, j, pt: (i, 0, 0))
    grid_spec = pltpu.PrefetchScalarGridSpec(
        num_scalar_prefetch=1,
        grid=(b, nchunks),
        in_specs=[per_b(rows, ATT_WIDTH), per_b(ATT_WIDTH, page), per_b(ATT_WIDTH, page),
                  pl.BlockSpec(memory_space=pltpu.SMEM)]
                 + [page_spec(p) for p in range(ppc)] * 2,
        out_specs=per_b(rows, ATT_WIDTH),
        scratch_shapes=[pltpu.VMEM((N_HEADS, rows, HEAD_DIM), F32),
                        pltpu.VMEM((2 * page, 2 * page), BF16),
                        pltpu.VMEM((N_HEADS, rows, page), F32),
                        pltpu.VMEM((N_HEADS, rows, HEAD_DIM), F32)],
    )
    return pl.pallas_call(
        body,
        grid_spec=grid_spec,
        out_shape=jax.ShapeDtypeStruct(q.shape, F32),
        compiler_params=pltpu.CompilerParams(
            dimension_semantics=("arbitrary", "arbitrary"), vmem_limit_bytes=VMEM_LIMIT_BYTES),
        name="sample_attention",
    )(page_table, q, knt, vnt, sb_bias, *([cache_kt] * ppc), *([cache_vt] * ppc))


def _merge_body(x_ref, cact_ref, o_ref, wg_ref, wcp_ref, wap_ref, wout_ref, g_ref, b_ref,
                x1_ref, *, alpha, d):
    x = x_ref[...]
    gates = jnp.dot(x.astype(BF16), wg_ref[...], preferred_element_type=F32)
    conv_out = jnp.dot(cact_ref[...], wcp_ref[...], preferred_element_type=F32)
    att_out = jnp.dot(o_ref[...], wap_ref[...], preferred_element_type=F32)
    m = jax.nn.sigmoid(gates[:, :d]) * conv_out + jax.nn.sigmoid(gates[:, d:]) * att_out
    mixed = jnp.dot(m.astype(BF16), wout_ref[...], preferred_element_type=F32)
    x1_ref[...] = _layer_norm(alpha * x + mixed, g_ref[...], b_ref[...])


def _merge(x, cact, o, w_g, w_cp, w_ap, w_out, ln_g, ln_b, *, tm, alpha):
    n, d = x.shape
    assert n % tm == 0
    body = functools.partial(_merge_body, alpha=alpha, d=d)
    tile = lambda w: pl.BlockSpec((tm, w), lambda i: (i, 0))
    return pl.pallas_call(
        body,
        grid=(n // tm,),
        in_specs=[tile(d), tile(cact.shape[1]), tile(o.shape[1]),
                  _const_spec(w_g.shape), _const_spec(w_cp.shape), _const_spec(w_ap.shape),
                  _const_spec(w_out.shape), _const_spec((1, d)), _const_spec((1, d))],
        out_specs=tile(d),
        out_shape=jax.ShapeDtypeStruct((n, d), F32),
        compiler_params=pltpu.CompilerParams(
            dimension_semantics=("arbitrary",), vmem_limit_bytes=VMEM_LIMIT_BYTES),
        name="merge",
    )(x, cact, o, w_g, w_cp, w_ap, w_out, ln_g.reshape(1, d), ln_b.reshape(1, d))


def _ffn_body(x1_ref, p_ref, wup_ref, wdn_ref, wpg_ref, wple_ref, g_ref, b_ref, y_ref,
              *, alpha, hidden, chunk):
    x1 = x1_ref[...]
    x1b = x1.astype(BF16)
    ple = (jax.nn.sigmoid(jnp.dot(x1b, wpg_ref[...], preferred_element_type=F32))
           * jnp.dot(p_ref[...].astype(BF16), wple_ref[...], preferred_element_type=F32))
    acc = alpha * x1 + ple
    for c0 in range(0, hidden, chunk):
        gate = jnp.dot(x1b, wup_ref[:, pl.ds(c0, chunk)], preferred_element_type=F32)
        up = jnp.dot(x1b, wup_ref[:, pl.ds(hidden + c0, chunk)], preferred_element_type=F32)
        act = (gate * jax.nn.sigmoid(gate) * up).astype(BF16)
        acc = acc + jnp.dot(act, wdn_ref[pl.ds(c0, chunk), :], preferred_element_type=F32)
    y_ref[...] = _layer_norm(acc, g_ref[...], b_ref[...])


def _ffn(x1, p, w_up, w_dn, w_pg, w_ple, ln_g, ln_b, *, tm, alpha, chunk):
    n, d = x1.shape
    hidden = w_dn.shape[0]
    assert n % tm == 0 and hidden % chunk == 0
    body = functools.partial(_ffn_body, alpha=alpha, hidden=hidden, chunk=chunk)
    tile = lambda w: pl.BlockSpec((tm, w), lambda i: (i, 0))
    return pl.pallas_call(
        body,
        grid=(n // tm,),
        in_specs=[tile(d), tile(p.shape[1]),
                  _const_spec(w_up.shape), _const_spec(w_dn.shape), _const_spec(w_pg.shape),
                  _const_spec(w_ple.shape), _const_spec((1, d)), _const_spec((1, d))],
        out_specs=tile(d),
        out_shape=jax.ShapeDtypeStruct((n, d), F32),
        compiler_params=pltpu.CompilerParams(
            dimension_semantics=("arbitrary",), vmem_limit_bytes=VMEM_LIMIT_BYTES),
        name="ffn",
    )(x1, p, w_up, w_dn, w_pg, w_ple, ln_g.reshape(1, d), ln_b.reshape(1, d))


def _pick_tile(n, candidates):
    for c in candidates:
        if n % c == 0:
            return c
    return n


def _layer(x_p, x_s, p_p, p_s, cache_kt, cache_vt, layer, state_conv, page_table, w, alpha):
    (w_in, sb_bias, conv_w, conv_b, conv_ln_g, conv_ln_b, w_conv_proj, w_att_proj, w_out,
     ln1_g, ln1_b, w_ffn_up, w_ffn_down, w_ple_gate, w_ple, ln2_g, ln2_b) = w
    bp, tp, d = x_p.shape
    bs, ts, _ = x_s.shape
    c = conv_w.shape[1]
    n_mix = 2 * c + 3 * ATT_WIDTH
    w_a = w_in[:, :n_mix].astype(BF16)
    w_g = w_in[:, n_mix:].astype(BF16)
    w_cp, w_ap, w_o = (a.astype(BF16) for a in (w_conv_proj, w_att_proj, w_out))
    w_up, w_dn, w_pg, w_pl = (a.astype(BF16) for a in (w_ffn_up, w_ffn_down, w_ple_gate, w_ple))
    page = cache_kt.shape[4]

    tm_p = _pick_tile(tp, (256, 128, 64, 32, 16, 8))
    cact_p, q_p, k_p, v_p, kb_p, vb_p, conv_p = _inproj_conv(
        x_p, w_a, jnp.zeros((bp, HIST_PAD, c), F32), conv_w, conv_b, conv_ln_g, conv_ln_b,
        bb=1, tm=tm_p, valid_last=tm_p)
    blk = _pick_tile(tp, (256, 128))
    o_p = _prompt_attention(sb_bias, q_p, kb_p, vb_p, blk=blk)

    ts_pad = -(-ts // 8) * 8
    x_s_pad = jnp.pad(x_s, ((0, 0), (0, ts_pad - ts), (0, 0)))
    past_pad = jnp.pad(state_conv, ((0, 0), (HIST_PAD - HIST, 0), (0, 0)))
    cact_s, q_s, k_s, v_s, kb_s, vb_s, conv_s = _inproj_conv(
        x_s_pad, w_a, past_pad, conv_w, conv_b, conv_ln_g, conv_ln_b,
        bb=_pick_tile(bs, (32, 16, 8, 4, 2, 1)), tm=ts_pad, valid_last=ts)
    knt = jnp.pad(jnp.swapaxes(kb_s, 1, 2), ((0, 0), (0, 0), (0, page - ts_pad)))
    vnt = jnp.pad(jnp.swapaxes(vb_s, 1, 2), ((0, 0), (0, 0), (0, page - ts_pad)))
    ppc = _pick_tile(page_table.shape[1], (16, 8, 4, 2, 1))
    o_s = _sample_attention(page_table, q_s, knt, vnt, sb_bias, cache_kt, cache_vt, layer, ppc=ppc)

    def tail(x, cact, o, p):
        n = x.shape[0]
        tm = _pick_tile(n, (512, 256, 128, 64, 32, 16, 8))
        x1 = _merge(x, cact, o, w_g, w_cp, w_ap, w_o, ln1_g, ln1_b, tm=tm, alpha=alpha)
        return _ffn(x1, p, w_up, w_dn, w_pg, w_pl, ln2_g, ln2_b, tm=tm, alpha=alpha, chunk=256)

    y_p = tail(x_p.reshape(bp * tp, d), cact_p.reshape(bp * tp, c),
               o_p.reshape(bp * tp, ATT_WIDTH), p_p.reshape(bp * tp, -1)).reshape(bp, tp, d)
    y_s = tail(x_s.reshape(bs * ts, d), cact_s[:, :ts].reshape(bs * ts, c),
               o_s[:, :ts].reshape(bs * ts, ATT_WIDTH).astype(BF16), p_s.reshape(bs * ts, -1)).reshape(bs, ts, d)

    heads = lambda a, b, t: a[:, :t].reshape(b, t, N_HEADS, HEAD_DIM)
    return (y_p, y_s, heads(k_p, bp, tp), heads(v_p, bp, tp), conv_p,
            heads(k_s, bs, ts), heads(v_s, bs, ts), conv_s)


def kernel(x_prompt, x_sample, p_prompt, p_sample, cache_k, cache_v, state_conv, page_table, w_in, sb_bias, conv_w, conv_b, conv_ln_g, conv_ln_b, w_conv_proj, w_att_proj, w_out, ln1_g, ln1_b, w_ffn_up, w_ffn_down, w_ple_gate, w_ple, ln2_g, ln2_b):
    depth = w_in.shape[0]
    alpha = (2.0 * depth) ** 0.25
    y_p, y_s = x_prompt, x_sample
    outs = [[] for _ in range(6)]
    cache_kt = jnp.transpose(cache_k, (0, 1, 3, 4, 2))
    cache_vt = jnp.transpose(cache_v, (0, 1, 3, 4, 2))
    for i in range(depth):
        w = (w_in[i], sb_bias[i], conv_w[i], conv_b[i], conv_ln_g[i], conv_ln_b[i],
             w_conv_proj[i], w_att_proj[i], w_out[i], ln1_g[i], ln1_b[i], w_ffn_up[i],
             w_ffn_down[i], w_ple_gate[i], w_ple[i], ln2_g[i], ln2_b[i])
        y_p, y_s, k1, v1, c1, k2, v2, c2 = _layer(
            y_p, y_s, p_prompt[i], p_sample[i], cache_kt, cache_vt, i, state_conv[i],
            page_table, w, alpha)
        for lst, a in zip(outs, (k1, v1, c1, k2, v2, c2)):
            lst.append(a)
    return (y_p, y_s) + tuple(jnp.stack(lst) for lst in outs)
```

```python
import functools

import jax
import jax.numpy as jnp
from jax import lax
from jax.experimental import pallas as pl
from jax.experimental.pallas import tpu as pltpu

F32 = jnp.float32
BF16 = jnp.bfloat16

N_HEADS = 8
HEAD_DIM = 64
ATT_WIDTH = N_HEADS * HEAD_DIM
CONV_WIDTH = 31
HIST = CONV_WIDTH - 1
HIST_PAD = 32
LN_EPS = 1e-5
LANES = 128
SUBLANES = 8
HEADS_PER_LANE_TILE = LANES // HEAD_DIM
VMEM_LIMIT_BYTES = 56 * 1024 * 1024
NEG_BIG = -1e30
LOG2E = 1.4426950408889634


def _layer_norm(x, g, b):
    mu = jnp.mean(x, axis=-1, keepdims=True)
    xc = x - mu
    var = jnp.mean(xc * xc, axis=-1, keepdims=True)
    return xc * lax.rsqrt(var + LN_EPS) * g + b


def _softplus2(z2):
    return jnp.maximum(z2, 0.0) + jnp.log2(1.0 + jnp.exp2(-jnp.abs(z2)))


def _split_bf16(x):
    hi = x.astype(BF16)
    lo = (x - hi.astype(F32)).astype(BF16)
    return hi, lo


def _tri_ge(n):
    j = lax.broadcasted_iota(jnp.int32, (n, n), 0)
    s = lax.broadcasted_iota(jnp.int32, (n, n), 1)
    return (j >= s).astype(BF16)


def _const_spec(shape):
    nd = len(shape)
    return pl.BlockSpec(shape, lambda *_: (0,) * nd, pipeline_mode=pl.Buffered(1))


def _conv_taps(ub, cw_ref, cb_ref, tm):
    acc = jnp.zeros((tm, ub.shape[1]), F32) + cb_ref[...]
    for rho in range(SUBLANES):
        taps = [kk for kk in range(CONV_WIDTH) if (HIST_PAD - HIST + kk) % SUBLANES == rho]
        ext = SUBLANES if rho else 0
        part = None
        for kk in taps:
            term = ub[pl.ds(HIST_PAD - HIST + kk - rho, tm + ext), :] * cw_ref[pl.ds(kk, 1), :]
            part = term if part is None else part + term
        acc = acc + part[rho:rho + tm, :]
    return acc


def _inproj_conv_body(x_ref, w_ref, past_ref, cw_ref, cb_ref, lg_ref, lb_ref,
                      cact_ref, q_ref, k_ref, v_ref, kb_ref, vb_ref, nc_ref, ubuf,
                      *, bb, tm, nt, valid_last, conv_ch):
    t = pl.program_id(1)
    d = x_ref.shape[2]

    @pl.when(t == 0)
    def _():
        ubuf[:, pl.ds(0, HIST_PAD), :] = past_ref[...]

    xb = x_ref[...].reshape(bb * tm, d).astype(BF16)
    h = jnp.dot(xb, w_ref[...], preferred_element_type=F32)
    c = conv_ch
    u = h[:, :c] * jax.nn.sigmoid(h[:, c:2 * c])
    q = h[:, 2 * c:2 * c + ATT_WIDTH]
    k = h[:, 2 * c + ATT_WIDTH:2 * c + 2 * ATT_WIDTH]
    v = h[:, 2 * c + 2 * ATT_WIDTH:2 * c + 3 * ATT_WIDTH]
    per_seq = lambda a: a.reshape(bb, tm, a.shape[1])
    q_ref[...] = per_seq((q * (HEAD_DIM ** -0.5)).astype(BF16))
    k_ref[...] = per_seq(k)
    v_ref[...] = per_seq(v)
    kb_ref[...] = per_seq(k.astype(BF16))
    vb_ref[...] = per_seq(v.astype(BF16))

    ubuf[:, pl.ds(HIST_PAD, tm), :] = per_seq(u)
    acc = [_conv_taps(ubuf.at[s], cw_ref, cb_ref, tm) for s in range(bb)]
    acc = acc[0] if bb == 1 else jnp.concatenate(acc, axis=0)
    y = _layer_norm(acc, lg_ref[...], lb_ref[...])
    cact_ref[...] = per_seq((y * jax.nn.sigmoid(y)).astype(BF16))

    @pl.when(t == nt - 1)
    def _():
        nc_ref[...] = ubuf[:, pl.ds(HIST_PAD + valid_last - HIST, HIST), :]

    if nt > 1:
        ubuf[:, pl.ds(0, HIST_PAD), :] = ubuf[:, pl.ds(tm, HIST_PAD), :]


def _inproj_conv(x, w_a, past_pad, conv_w, conv_b, ln_g, ln_b, *, bb, tm, valid_last):
    b, t, d = x.shape
    c = conv_w.shape[1]
    nt = t // tm
    assert nt * tm == t and tm % SUBLANES == 0 and b % bb == 0
    body = functools.partial(_inproj_conv_body, bb=bb, tm=tm, nt=nt, valid_last=valid_last,
                             conv_ch=c)
    tile = lambda w: pl.BlockSpec((bb, tm, w), lambda i, j: (i, j, 0))
    per_b = lambda r, w: pl.BlockSpec((bb, r, w), lambda i, j: (i, 0, 0))
    return pl.pallas_call(
        body,
        grid=(b // bb, nt),
        in_specs=[tile(d), _const_spec(w_a.shape), per_b(HIST_PAD, c),
                  _const_spec(conv_w.shape), _const_spec((1, c)), _const_spec((1, c)),
                  _const_spec((1, c))],
        out_specs=[tile(c), tile(ATT_WIDTH), tile(ATT_WIDTH), tile(ATT_WIDTH), tile(ATT_WIDTH),
                   tile(ATT_WIDTH), per_b(HIST, c)],
        out_shape=[jax.ShapeDtypeStruct((b, t, c), BF16),
                   jax.ShapeDtypeStruct((b, t, ATT_WIDTH), BF16),
                   jax.ShapeDtypeStruct((b, t, ATT_WIDTH), F32),
                   jax.ShapeDtypeStruct((b, t, ATT_WIDTH), F32),
                   jax.ShapeDtypeStruct((b, t, ATT_WIDTH), BF16),
                   jax.ShapeDtypeStruct((b, t, ATT_WIDTH), BF16),
                   jax.ShapeDtypeStruct((b, HIST, c), F32)],
        scratch_shapes=[pltpu.VMEM((bb, HIST_PAD + tm, c), F32)],
        compiler_params=pltpu.CompilerParams(
            dimension_semantics=("arbitrary", "arbitrary"), vmem_limit_bytes=VMEM_LIMIT_BYTES),
        name="inproj_conv",
    )(x, w_a, past_pad, conv_w, conv_b.reshape(1, c), ln_g.reshape(1, c), ln_b.reshape(1, c))


def _prompt_attn_body(bias_ref, q_ref, k_ref, v_ref, o_ref, qm_ref, tri_ref, acc_ref, *, blk):
    qi = pl.program_id(1)
    row = lax.broadcasted_iota(jnp.int32, (blk, blk), 0)
    col = lax.broadcasted_iota(jnp.int32, (blk, blk), 1)
    lane = lax.broadcasted_iota(jnp.int32, (blk, LANES), 1)
    dims = (((1,), (1,)), ((), ()))
    tri_ref[...] = _tri_ge(blk)
    for h in range(N_HEADS):
        p, hh = divmod(h, HEADS_PER_LANE_TILE)
        qp = q_ref[0, :, pl.ds(p * LANES, LANES)]
        in_head = (lane >= hh * HEAD_DIM) & (lane < (hh + 1) * HEAD_DIM)
        qm_ref[h] = jnp.where(in_head, qp, jnp.zeros_like(qp))

    heads = range(N_HEADS)

    def key_blocks(lo_block, nb, carries, causal):
        keys = pl.ds(pl.multiple_of(lo_block * blk, blk), nb * blk)
        lanes = [pl.ds((h // HEADS_PER_LANE_TILE) * LANES, LANES) for h in heads]
        z2 = [(lax.dot_general(qm_ref[h], k_ref[0, keys, lanes[h]], dims,
                               preferred_element_type=F32) + bias_ref[h]) * LOG2E for h in heads]
        z2 = [[z[:, i * blk:(i + 1) * blk] for i in range(nb)] for z in z2]
        lhs = []
        for h in heads:
            for i in range(nb):
                sp = _softplus2(z2[h][i])
                if causal is not None and i == nb - 1:
                    sp = jnp.where(causal, sp, 0.0)
                lhs.append(sp.astype(BF16))
        local = [jnp.dot(x, tri_ref[...], preferred_element_type=F32) for x in lhs]
        a, new_carries = [], []
        for h in heads:
            carry = None if carries is None else carries[h]
            tiles = [None] * nb
            for i in reversed(range(nb)):
                cs = local[h * nb + i] if carry is None else local[h * nb + i] + carry
                carry = cs[:, 0:1]
                arg = z2[h][i] - cs
                if causal is not None and i == nb - 1:
                    arg = jnp.where(causal, arg, NEG_BIG)
                tiles[i] = jnp.exp2(arg).astype(BF16)
            a.append(tiles[0] if nb == 1 else jnp.concatenate(tiles, axis=1))
            new_carries.append(carry)
        for h in heads:
            pv = jnp.dot(a[h], v_ref[0, keys, lanes[h]], preferred_element_type=F32)
            if carries is None:
                acc_ref[h] = pv
            else:
                acc_ref[h] += pv
        return tuple(new_carries)

    odd = qi % 2
    carries = lax.cond(odd == 0,
                       lambda: key_blocks(qi, 1, None, col < row),
                       lambda: key_blocks(qi - 1, 2, None, col < row))
    top = qi - 1 - odd
    lax.fori_loop(0, qi // 2, lambda jj, c: key_blocks(top - 1 - 2 * jj, 2, c, None), carries)
    for p in range(N_HEADS // HEADS_PER_LANE_TILE):
        o_pair = jnp.where(lane < HEAD_DIM, acc_ref[HEADS_PER_LANE_TILE * p],
                           acc_ref[HEADS_PER_LANE_TILE * p + 1])
        o_ref[0, :, pl.ds(p * LANES, LANES)] = o_pair.astype(BF16)


def _prompt_attention(sb_bias, q, kb, vb, *, blk):
    b, t, w = q.shape
    nq = t // blk
    assert nq * blk == t and HEADS_PER_LANE_TILE == 2
    body = functools.partial(_prompt_attn_body, blk=blk)
    return pl.pallas_call(
        body,
        grid=(b, nq),
        in_specs=[pl.BlockSpec(memory_space=pltpu.SMEM),
                  pl.BlockSpec((1, blk, w), lambda i, j: (i, j, 0)),
                  pl.BlockSpec((1, t, w), lambda i, j: (i, 0, 0)),
                  pl.BlockSpec((1, t, w), lambda i, j: (i, 0, 0))],
        out_specs=pl.BlockSpec((1, blk, w), lambda i, j: (i, j, 0)),
        out_shape=jax.ShapeDtypeStruct((b, t, w), BF16),
        scratch_shapes=[pltpu.VMEM((N_HEADS, blk, LANES), BF16),
                        pltpu.VMEM((blk, blk), BF16),
                        pltpu.VMEM((N_HEADS, blk, LANES), F32)],
        compiler_params=pltpu.CompilerParams(
            dimension_semantics=("arbitrary", "arbitrary"), vmem_limit_bytes=VMEM_LIMIT_BYTES),
        name="prompt_attention",
    )(sb_bias, q, kb, vb)


def _sample_attn_body(pt_ref, q_ref, knt_ref, vnt_ref, bias_ref, *rest, ppc, page):
    k_refs = rest[:ppc]
    v_refs = rest[ppc:2 * ppc]
    o_ref = rest[2 * ppc]
    qh_ref, tri_ref, carry_ref, acc_ref = rest[2 * ppc + 1:]
    del pt_ref
    c = pl.program_id(1)
    nc = pl.num_programs(1)
    rows = q_ref.shape[1]
    dims_nt = (((1,), (1,)), ((), ()))
    heads = range(N_HEADS)

    def process(kt_heads, vt_heads, n, valid):
        z2 = [(jnp.dot(qh_ref[h].astype(BF16), kt_heads[h], preferred_element_type=F32)
               + bias_ref[h]) * LOG2E for h in heads]
        zs = jnp.concatenate([z2[h][:, p * page:(p + 1) * page] for h in heads for p in range(n)],
                             axis=0)
        sp = _softplus2(zs)
        if valid is not None:
            sp = jnp.where(valid, sp, 0.0)
        r = jnp.dot(jnp.concatenate(_split_bf16(sp), axis=1), tri_ref[...],
                    preferred_element_type=F32)
        local, total = r[:, :page], r[:, page:]
        a = []
        for h in heads:
            carry = carry_ref[h]
            tiles = []
            for p in range(n):
                rs = slice((h * n + p) * rows, (h * n + p + 1) * rows)
                arg = zs[rs] - (local[rs] + carry)
                if valid is not None:
                    arg = jnp.where(valid[rs], arg, NEG_BIG)
                tiles.append(jnp.exp2(arg))
                carry = carry + total[rs]
            carry_ref[h] = carry
            a.append(jnp.concatenate(tiles, axis=1).astype(BF16))
        for h in heads:
            acc_ref[h] += lax.dot_general(a[h], vt_heads[h], dims_nt, preferred_element_type=F32)

    @pl.when(c == 0)
    def _():
        tri = _tri_ge(page)
        both = jnp.concatenate([tri, jnp.ones_like(tri)], axis=1)
        tri_ref[pl.ds(0, page), :] = both
        tri_ref[pl.ds(page, page), :] = both
        for h in heads:
            qh_ref[h] = q_ref[0, :, pl.ds(h * HEAD_DIM, HEAD_DIM)].astype(F32)
        carry_ref[...] = jnp.zeros_like(carry_ref)
        acc_ref[...] = jnp.zeros_like(acc_ref)
        key = lax.broadcasted_iota(jnp.int32, (N_HEADS * rows, page), 1)
        qidx = lax.broadcasted_iota(jnp.int32, (N_HEADS * rows, page), 0) % rows
        process([knt_ref[0, pl.ds(h * HEAD_DIM, HEAD_DIM), :] for h in heads],
                [vnt_ref[0, pl.ds(h * HEAD_DIM, HEAD_DIM), :] for h in heads], 1, key < qidx)

    def head_slab(refs, h):
        return jnp.concatenate([r[h] for r in refs], axis=1).astype(BF16)

    process([head_slab(k_refs, h) for h in heads], [head_slab(v_refs, h) for h in heads], ppc, None)

    @pl.when(c == nc - 1)
    def _():
        for h in heads:
            o_ref[0, :, pl.ds(h * HEAD_DIM, HEAD_DIM)] = acc_ref[h]


def _sample_attention(page_table, q, knt, vnt, sb_bias, cache_kt, cache_vt, layer, *, ppc):
    b, rows, _ = q.shape
    n_pages = page_table.shape[1]
    page = cache_kt.shape[4]
    nchunks = n_pages // ppc
    assert nchunks * ppc == n_pages and rows % 8 == 0 and page == LANES
    body = functools.partial(_sample_attn_body, ppc=ppc, page=page)

    def page_spec(p):
        return pl.BlockSpec(
            (None, None, N_HEADS, HEAD_DIM, page),
            lambda i, j, pt: (layer, pt[i, n_pages - 1 - (j * ppc + p)], 0, 0, 0))

    per_b = lambda r, w: pl.BlockSpec((1, r, w), lambda i, j, pt: (i, 0, 0))
    grid_spec = pltpu.PrefetchScalarGridSpec(
        num_scalar_prefetch=1,
        grid=(b, nchunks),
        in_specs=[per_b(rows, ATT_WIDTH), per_b(ATT_WIDTH, page), per_b(ATT_WIDTH, page),
                  pl.BlockSpec(memory_space=pltpu.SMEM)]
                 + [page_spec(p) for p in range(ppc)] * 2,
        out_specs=per_b(rows, ATT_WIDTH),
        scratch_shapes=[pltpu.VMEM((N_HEADS, rows, HEAD_DIM), F32),
                        pltpu.VMEM((2 * page, 2 * page), BF16),
                        pltpu.VMEM((N_HEADS, rows, page), F32),
                        pltpu.VMEM((N_HEADS, rows, HEAD_DIM), F32)],
    )
    return pl.pallas_call(
        body,
        grid_spec=grid_spec,
        out_shape=jax.ShapeDtypeStruct(q.shape, F32),
        compiler_params=pltpu.CompilerParams(
            dimension_semantics=("arbitrary", "arbitrary"), vmem_limit_bytes=VMEM_LIMIT_BYTES),
        name="sample_attention",
    )(page_table, q, knt, vnt, sb_bias, *([cache_kt] * ppc), *([cache_vt] * ppc))


def _merge_body(x_ref, cact_ref, o_ref, wg_ref, wcp_ref, wap_ref, wout_ref, g_ref, b_ref,
                x1_ref, *, alpha, d):
    x = x_ref[...]
    gates = jnp.dot(x.astype(BF16), wg_ref[...], preferred_element_type=F32)
    conv_out = jnp.dot(cact_ref[...], wcp_ref[...], preferred_element_type=F32)
    att_out = jnp.dot(o_ref[...], wap_ref[...], preferred_element_type=F32)
    m = jax.nn.sigmoid(gates[:, :d]) * conv_out + jax.nn.sigmoid(gates[:, d:]) * att_out
    mixed = jnp.dot(m.astype(BF16), wout_ref[...], preferred_element_type=F32)
    x1_ref[...] = _layer_norm(alpha * x + mixed, g_ref[...], b_ref[...])


def _merge(x, cact, o, w_g, w_cp, w_ap, w_out, ln_g, ln_b, *, tm, alpha):
    n, d = x.shape
    assert n % tm == 0
    body = functools.partial(_merge_body, alpha=alpha, d=d)
    tile = lambda w: pl.BlockSpec((tm, w), lambda i: (i, 0))
    return pl.pallas_call(
        body,
        grid=(n // tm,),
        in_specs=[tile(d), tile(cact.shape[1]), tile(o.shape[1]),
                  _const_spec(w_g.shape), _const_spec(w_cp.shape), _const_spec(w_ap.shape),
                  _const_spec(w_out.shape), _const_spec((1, d)), _const_spec((1, d))],
        out_specs=tile(d),
        out_shape=jax.ShapeDtypeStruct((n, d), F32),
        compiler_params=pltpu.CompilerParams(
            dimension_semantics=("arbitrary",), vmem_limit_bytes=VMEM_LIMIT_BYTES),
        name="merge",
    )(x, cact, o, w_g, w_cp, w_ap, w_out, ln_g.reshape(1, d), ln_b.reshape(1, d))


def _ffn_body(x1_ref, p_ref, wup_ref, wdn_ref, wpg_ref, wple_ref, g_ref, b_ref, y_ref,
              *, alpha, hidden, chunk):
    x1 = x1_ref[...]
    x1b = x1.astype(BF16)
    ple = (jax.nn.sigmoid(jnp.dot(x1b, wpg_ref[...], preferred_element_type=F32))
           * jnp.dot(p_ref[...].astype(BF16), wple_ref[...], preferred_element_type=F32))
    acc = alpha * x1 + ple
    for c0 in range(0, hidden, chunk):
        gate = jnp.dot(x1b, wup_ref[:, pl.ds(c0, chunk)], preferred_element_type=F32)
        up = jnp.dot(x1b, wup_ref[:, pl.ds(hidden + c0, chunk)], preferred_element_type=F32)
        act = (gate * jax.nn.sigmoid(gate) * up).astype(BF16)
        acc = acc + jnp.dot(act, wdn_ref[pl.ds(c0, chunk), :], preferred_element_type=F32)
    y_ref[...] = _layer_norm(acc, g_ref[...], b_ref[...])


def _ffn(x1, p, w_up, w_dn, w_pg, w_ple, ln_g, ln_b, *, tm, alpha, chunk):
    n, d = x1.shape
    hidden = w_dn.shape[0]
    assert n % tm == 0 and hidden % chunk == 0
    body = functools.partial(_ffn_body, alpha=alpha, hidden=hidden, chunk=chunk)
    tile = lambda w: pl.BlockSpec((tm, w), lambda i: (i, 0))
    return pl.pallas_call(
        body,
        grid=(n // tm,),
        in_specs=[tile(d), tile(p.shape[1]),
                  _const_spec(w_up.shape), _const_spec(w_dn.shape), _const_spec(w_pg.shape),
                  _const_spec(w_ple.shape), _const_spec((1, d)), _const_spec((1, d))],
        out_specs=tile(d),
        out_shape=jax.ShapeDtypeStruct((n, d), F32),
        compiler_params=pltpu.CompilerParams(
            dimension_semantics=("arbitrary",), vmem_limit_bytes=VMEM_LIMIT_BYTES),
        name="ffn",
    )(x1, p, w_up, w_dn, w_pg, w_ple, ln_g.reshape(1, d), ln_b.reshape(1, d))


def _pick_tile(n, candidates):
    for c in candidates:
        if n % c == 0:
            return c
    return n


def _layer(x_p, x_s, p_p, p_s, cache_kt, cache_vt, layer, state_conv, page_table, w, alpha):
    (w_in, sb_bias, conv_w, conv_b, conv_ln_g, conv_ln_b, w_conv_proj, w_att_proj, w_out,
     ln1_g, ln1_b, w_ffn_up, w_ffn_down, w_ple_gate, w_ple, ln2_g, ln2_b) = w
    bp, tp, d = x_p.shape
    bs, ts, _ = x_s.shape
    c = conv_w.shape[1]
    n_mix = 2 * c + 3 * ATT_WIDTH
    w_a = w_in[:, :n_mix].astype(BF16)
    w_g = w_in[:, n_mix:].astype(BF16)
    w_cp, w_ap, w_o = (a.astype(BF16) for a in (w_conv_proj, w_att_proj, w_out))
    w_up, w_dn, w_pg, w_pl = (a.astype(BF16) for a in (w_ffn_up, w_ffn_down, w_ple_gate, w_ple))
    page = cache_kt.shape[4]

    tm_p = _pick_tile(tp, (256, 128, 64, 32, 16, 8))
    cact_p, q_p, k_p, v_p, kb_p, vb_p, conv_p = _inproj_conv(
        x_p, w_a, jnp.zeros((bp, HIST_PAD, c), F32), conv_w, conv_b, conv_ln_g, conv_ln_b,
        bb=1, tm=tm_p, valid_last=tm_p)
    blk = _pick_tile(tp, (256, 128))
    o_p = _prompt_attention(sb_bias, q_p, kb_p, vb_p, blk=blk)

    ts_pad = -(-ts // 8) * 8
    x_s_pad = jnp.pad(x_s, ((0, 0), (0, ts_pad - ts), (0, 0)))
    past_pad = jnp.pad(state_conv, ((0, 0), (HIST_PAD - HIST, 0), (0, 0)))
    cact_s, q_s, k_s, v_s, kb_s, vb_s, conv_s = _inproj_conv(
        x_s_pad, w_a, past_pad, conv_w, conv_b, conv_ln_g, conv_ln_b,
        bb=_pick_tile(bs, (32, 16, 8, 4, 2, 1)), tm=ts_pad, valid_last=ts)
    knt = jnp.pad(jnp.swapaxes(kb_s, 1, 2), ((0, 0), (0, 0), (0, page - ts_pad)))
    vnt = jnp.pad(jnp.swapaxes(vb_s, 1, 2), ((0, 0), (0, 0), (0, page - ts_pad)))
    ppc = _pick_tile(page_table.shape[1], (32, 16, 8, 4, 2, 1))
    o_s = _sample_attention(page_table, q_s, knt, vnt, sb_bias, cache_kt, cache_vt, layer, ppc=ppc)

    def tail(x, cact, o, p):
        n = x.shape[0]
        tm = _pick_tile(n, (512, 256, 128, 64, 32, 16, 8))
        x1 = _merge(x, cact, o, w_g, w_cp, w_ap, w_o, ln1_g, ln1_b, tm=tm, alpha=alpha)
        return _ffn(x1, p, w_up, w_dn, w_pg, w_pl, ln2_g, ln2_b, tm=tm, alpha=alpha, chunk=256)

    y_p = tail(x_p.reshape(bp * tp, d), cact_p.reshape(bp * tp, c),
               o_p.reshape(bp * tp, ATT_WIDTH), p_p.reshape(bp * tp, -1)).reshape(bp, tp, d)
    y_s = tail(x_s.reshape(bs * ts, d), cact_s[:, :ts].reshape(bs * ts, c),
               o_s[:, :ts].reshape(bs * ts, ATT_WIDTH).astype(BF16), p_s.reshape(bs * ts, -1)).reshape(bs, ts, d)

    heads = lambda a, b, t: a[:, :t].reshape(b, t, N_HEADS, HEAD_DIM)
    return (y_p, y_s, heads(k_p, bp, tp), heads(v_p, bp, tp), conv_p,
            heads(k_s, bs, ts), heads(v_s, bs, ts), conv_s)


def kernel(x_prompt, x_sample, p_prompt, p_sample, cache_k, cache_v, state_conv, page_table, w_in, sb_bias, conv_w, conv_b, conv_ln_g, conv_ln_b, w_conv_proj, w_att_proj, w_out, ln1_g, ln1_b, w_ffn_up, w_ffn_down, w_ple_gate, w_ple, ln2_g, ln2_b):
    depth = w_in.shape[0]
    alpha = (2.0 * depth) ** 0.25
    y_p, y_s = x_prompt, x_sample
    outs = [[] for _ in range(6)]
    cache_kt = jnp.transpose(cache_k, (0, 1, 3, 4, 2))
    cache_vt = jnp.transpose(cache_v, (0, 1, 3, 4, 2))
    for i in range(depth):
        w = (w_in[i], sb_bias[i], conv_w[i], conv_b[i], conv_ln_g[i], conv_ln_b[i],
             w_conv_proj[i], w_att_proj[i], w_out[i], ln1_g[i], ln1_b[i], w_ffn_up[i],
             w_ffn_down[i], w_ple_gate[i], w_ple[i], ln2_g[i], ln2_b[i])
        y_p, y_s, k1, v1, c1, k2, v2, c2 = _layer(
            y_p, y_s, p_prompt[i], p_sample[i], cache_kt, cache_vt, i, state_conv[i],
            page_table, w, alpha)
        for lst, a in zip(outs, (k1, v1, c1, k2, v2, c2)):
            lst.append(a)
    return (y_p, y_s) + tuple(jnp.stack(lst) for lst in outs)
```

```python
import functools

import jax
import jax.numpy as jnp
from jax import lax
from jax.experimental import pallas as pl
from jax.experimental.pallas import tpu as pltpu

F32 = jnp.float32
BF16 = jnp.bfloat16

N_HEADS = 8
HEAD_DIM = 64
ATT_WIDTH = N_HEADS * HEAD_DIM
CONV_WIDTH = 31
HIST = CONV_WIDTH - 1
HIST_PAD = 32
LN_EPS = 1e-5
LANES = 128
SUBLANES = 8
HEADS_PER_LANE_TILE = LANES // HEAD_DIM
VMEM_LIMIT_BYTES = 56 * 1024 * 1024
NEG_BIG = -1e30
LOG2E = 1.4426950408889634


def _layer_norm(x, g, b):
    mu = jnp.mean(x, axis=-1, keepdims=True)
    xc = x - mu
    var = jnp.mean(xc * xc, axis=-1, keepdims=True)
    return xc * lax.rsqrt(var + LN_EPS) * g + b


def _softplus2(z2):
    neg_abs = lax.bitcast_convert_type(
        lax.bitcast_convert_type(z2, jnp.uint32) | jnp.uint32(0x80000000), F32)
    return jnp.maximum(z2, 0.0) + jnp.log2(1.0 + jnp.exp2(neg_abs))


def _split_bf16(x):
    hi = x.astype(BF16)
    lo = (x - hi.astype(F32)).astype(BF16)
    return hi, lo


def _tri_ge(n):
    j = lax.broadcasted_iota(jnp.int32, (n, n), 0)
    s = lax.broadcasted_iota(jnp.int32, (n, n), 1)
    return (j >= s).astype(BF16)


def _const_spec(shape):
    nd = len(shape)
    return pl.BlockSpec(shape, lambda *_: (0,) * nd, pipeline_mode=pl.Buffered(1))


def _conv_taps(ub, cw_ref, cb_ref, tm):
    acc = jnp.zeros((tm, ub.shape[1]), F32) + cb_ref[...]
    for rho in range(SUBLANES):
        taps = [kk for kk in range(CONV_WIDTH) if (HIST_PAD - HIST + kk) % SUBLANES == rho]
        ext = SUBLANES if rho else 0
        part = None
        for kk in taps:
            term = ub[pl.ds(HIST_PAD - HIST + kk - rho, tm + ext), :] * cw_ref[pl.ds(kk, 1), :]
            part = term if part is None else part + term
        acc = acc + part[rho:rho + tm, :]
    return acc


def _inproj_conv_body(x_ref, w_ref, past_ref, cw_ref, cb_ref, lg_ref, lb_ref,
                      cact_ref, q_ref, k_ref, v_ref, kb_ref, vb_ref, nc_ref, ubuf,
                      *, bb, tm, nt, valid_last, conv_ch):
    t = pl.program_id(1)
    d = x_ref.shape[2]

    @pl.when(t == 0)
    def _():
        ubuf[:, pl.ds(0, HIST_PAD), :] = past_ref[...]

    xb = x_ref[...].reshape(bb * tm, d).astype(BF16)
    h = jnp.dot(xb, w_ref[:, pl.ds(0, 2 * conv_ch + 3 * ATT_WIDTH)], preferred_element_type=F32)
    c = conv_ch
    u = h[:, :c] * jax.nn.sigmoid(h[:, c:2 * c])
    q = h[:, 2 * c:2 * c + ATT_WIDTH]
    k = h[:, 2 * c + ATT_WIDTH:2 * c + 2 * ATT_WIDTH]
    v = h[:, 2 * c + 2 * ATT_WIDTH:2 * c + 3 * ATT_WIDTH]
    per_seq = lambda a: a.reshape(bb, tm, a.shape[1])
    q_ref[...] = per_seq((q * (HEAD_DIM ** -0.5)).astype(BF16))
    k_ref[...] = per_seq(k)
    v_ref[...] = per_seq(v)
    kb_ref[...] = per_seq(k.astype(BF16))
    vb_ref[...] = per_seq(v.astype(BF16))

    ubuf[:, pl.ds(HIST_PAD, tm), :] = per_seq(u)
    acc = [_conv_taps(ubuf.at[s], cw_ref, cb_ref, tm) for s in range(bb)]
    acc = acc[0] if bb == 1 else jnp.concatenate(acc, axis=0)
    y = _layer_norm(acc, lg_ref[...], lb_ref[...])
    cact_ref[...] = per_seq((y * jax.nn.sigmoid(y)).astype(BF16))

    @pl.when(t == nt - 1)
    def _():
        nc_ref[...] = ubuf[:, pl.ds(HIST_PAD + valid_last - HIST, HIST), :]

    if nt > 1:
        ubuf[:, pl.ds(0, HIST_PAD), :] = ubuf[:, pl.ds(tm, HIST_PAD), :]


def _inproj_conv(x, w_a, past_pad, conv_w, conv_b, ln_g, ln_b, *, bb, tm, valid_last):
    b, t, d = x.shape
    c = conv_w.shape[1]
    nt = t // tm
    assert nt * tm == t and tm % SUBLANES == 0 and b % bb == 0
    body = functools.partial(_inproj_conv_body, bb=bb, tm=tm, nt=nt, valid_last=valid_last,
                             conv_ch=c)
    tile = lambda w: pl.BlockSpec((bb, tm, w), lambda i, j: (i, j, 0))
    per_b = lambda r, w: pl.BlockSpec((bb, r, w), lambda i, j: (i, 0, 0))
    return pl.pallas_call(
        body,
        grid=(b // bb, nt),
        in_specs=[tile(d), _const_spec(w_a.shape), per_b(HIST_PAD, c),
                  _const_spec(conv_w.shape), _const_spec((1, c)), _const_spec((1, c)),
                  _const_spec((1, c))],
        out_specs=[tile(c), tile(ATT_WIDTH), tile(ATT_WIDTH), tile(ATT_WIDTH), tile(ATT_WIDTH),
                   tile(ATT_WIDTH), per_b(HIST, c)],
        out_shape=[jax.ShapeDtypeStruct((b, t, c), BF16),
                   jax.ShapeDtypeStruct((b, t, ATT_WIDTH), BF16),
                   jax.ShapeDtypeStruct((b, t, ATT_WIDTH), F32),
                   jax.ShapeDtypeStruct((b, t, ATT_WIDTH), F32),
                   jax.ShapeDtypeStruct((b, t, ATT_WIDTH), BF16),
                   jax.ShapeDtypeStruct((b, t, ATT_WIDTH), BF16),
                   jax.ShapeDtypeStruct((b, HIST, c), F32)],
        scratch_shapes=[pltpu.VMEM((bb, HIST_PAD + tm, c), F32)],
        compiler_params=pltpu.CompilerParams(
            dimension_semantics=("arbitrary", "arbitrary"), vmem_limit_bytes=VMEM_LIMIT_BYTES),
        name="inproj_conv",
    )(x, w_a, past_pad, conv_w, conv_b.reshape(1, c), ln_g.reshape(1, c), ln_b.reshape(1, c))


def _prompt_attn_body(bias_ref, q_ref, k_ref, v_ref, o_ref, qm_ref, tri_ref, acc_ref, *, blk):
    qi = pl.program_id(1)
    row = lax.broadcasted_iota(jnp.int32, (blk, blk), 0)
    col = lax.broadcasted_iota(jnp.int32, (blk, blk), 1)
    lane = lax.broadcasted_iota(jnp.int32, (blk, LANES), 1)
    dims = (((1,), (1,)), ((), ()))
    tri_ref[...] = _tri_ge(blk)
    for h in range(N_HEADS):
        p, hh = divmod(h, HEADS_PER_LANE_TILE)
        qp = q_ref[0, :, pl.ds(p * LANES, LANES)]
        in_head = (lane >= hh * HEAD_DIM) & (lane < (hh + 1) * HEAD_DIM)
        qm_ref[h] = jnp.where(in_head, qp, jnp.zeros_like(qp))

    heads = range(N_HEADS)

    def key_blocks(lo_block, nb, carries, causal):
        keys = pl.ds(pl.multiple_of(lo_block * blk, blk), nb * blk)
        lanes = [pl.ds((h // HEADS_PER_LANE_TILE) * LANES, LANES) for h in heads]
        z2 = [(lax.dot_general(qm_ref[h], k_ref[0, keys, lanes[h]], dims,
                               preferred_element_type=F32) + bias_ref[h]) * LOG2E for h in heads]
        z2 = [[z[:, i * blk:(i + 1) * blk] for i in range(nb)] for z in z2]
        lhs = []
        for h in heads:
            for i in range(nb):
                sp = _softplus2(z2[h][i])
                if causal is not None and i == nb - 1:
                    sp = jnp.where(causal, sp, 0.0)
                lhs.append(sp.astype(BF16))
        local = [jnp.dot(x, tri_ref[...], preferred_element_type=F32) for x in lhs]
        a, new_carries = [], []
        for h in heads:
            carry = None if carries is None else carries[h]
            tiles = [None] * nb
            for i in reversed(range(nb)):
                cs = local[h * nb + i] if carry is None else local[h * nb + i] + carry
                carry = cs[:, 0:1]
                arg = z2[h][i] - cs
                if causal is not None and i == nb - 1:
                    arg = jnp.where(causal, arg, NEG_BIG)
                tiles[i] = jnp.exp2(arg).astype(BF16)
            a.append(tiles[0] if nb == 1 else jnp.concatenate(tiles, axis=1))
            new_carries.append(carry)
        for h in heads:
            pv = jnp.dot(a[h], v_ref[0, keys, lanes[h]], preferred_element_type=F32)
            if carries is None:
                acc_ref[h] = pv
            else:
                acc_ref[h] += pv
        return tuple(new_carries)

    odd = qi % 2
    carries = lax.cond(odd == 0,
                       lambda: key_blocks(qi, 1, None, col < row),
                       lambda: key_blocks(qi - 1, 2, None, col < row))
    top = qi - 1 - odd
    lax.fori_loop(0, qi // 2, lambda jj, c: key_blocks(top - 1 - 2 * jj, 2, c, None), carries)
    for p in range(N_HEADS // HEADS_PER_LANE_TILE):
        o_pair = jnp.where(lane < HEAD_DIM, acc_ref[HEADS_PER_LANE_TILE * p],
                           acc_ref[HEADS_PER_LANE_TILE * p + 1])
        o_ref[0, :, pl.ds(p * LANES, LANES)] = o_pair.astype(BF16)


def _prompt_attention(sb_bias, q, kb, vb, *, blk):
    b, t, w = q.shape
    nq = t // blk
    assert nq * blk == t and HEADS_PER_LANE_TILE == 2
    body = functools.partial(_prompt_attn_body, blk=blk)
    return pl.pallas_call(
        body,
        grid=(b, nq),
        in_specs=[pl.BlockSpec(memory_space=pltpu.SMEM),
                  pl.BlockSpec((1, blk, w), lambda i, j: (i, j, 0)),
                  pl.BlockSpec((1, t, w), lambda i, j: (i, 0, 0)),
                  pl.BlockSpec((1, t, w), lambda i, j: (i, 0, 0))],
        out_specs=pl.BlockSpec((1, blk, w), lambda i, j: (i, j, 0)),
        out_shape=jax.ShapeDtypeStruct((b, t, w), BF16),
        scratch_shapes=[pltpu.VMEM((N_HEADS, blk, LANES), BF16),
                        pltpu.VMEM((blk, blk), BF16),
                        pltpu.VMEM((N_HEADS, blk, LANES), F32)],
        compiler_params=pltpu.CompilerParams(
            dimension_semantics=("arbitrary", "arbitrary"), vmem_limit_bytes=VMEM_LIMIT_BYTES),
        name="prompt_attention",
    )(sb_bias, q, kb, vb)


def _sample_attn_body(pt_ref, q_ref, knt_ref, vnt_ref, bias_ref, *rest, ppc, page):
    k_refs = rest[:ppc]
    v_refs = rest[ppc:2 * ppc]
    o_ref = rest[2 * ppc]
    qh_ref, tri_ref, carry_ref, acc_ref = rest[2 * ppc + 1:]
    del pt_ref
    c = pl.program_id(1)
    nc = pl.num_programs(1)
    rows = q_ref.shape[1]
    dims_nt = (((1,), (1,)), ((), ()))
    heads = range(N_HEADS)

    def process(kt_heads, vt_heads, n, valid):
        z2 = [(jnp.dot(qh_ref[h].astype(BF16), kt_heads[h], preferred_element_type=F32)
               + bias_ref[h]) * LOG2E for h in heads]
        zs = jnp.concatenate([z2[h][:, p * page:(p + 1) * page] for h in heads for p in range(n)],
                             axis=0)
        sp = _softplus2(zs)
        if valid is not None:
            sp = jnp.where(valid, sp, 0.0)
        r = jnp.dot(jnp.concatenate(_split_bf16(sp), axis=1), tri_ref[...],
                    preferred_element_type=F32)
        local, total = r[:, :page], r[:, page:]
        a = []
        for h in heads:
            carry = carry_ref[h]
            tiles = []
            for p in range(n):
                rs = slice((h * n + p) * rows, (h * n + p + 1) * rows)
                arg = zs[rs] - (local[rs] + carry)
                if valid is not None:
                    arg = jnp.where(valid[rs], arg, NEG_BIG)
                tiles.append(jnp.exp2(arg))
                carry = carry + total[rs]
            carry_ref[h] = carry
            a.append(jnp.concatenate(tiles, axis=1).astype(BF16))
        for h in heads:
            acc_ref[h] += lax.dot_general(a[h], vt_heads[h], dims_nt, preferred_element_type=F32)

    @pl.when(c == 0)
    def _():
        tri = _tri_ge(page)
        both = jnp.concatenate([tri, jnp.ones_like(tri)], axis=1)
        tri_ref[pl.ds(0, page), :] = both
        tri_ref[pl.ds(page, page), :] = both
        for h in heads:
            qh_ref[h] = q_ref[0, :, pl.ds(h * HEAD_DIM, HEAD_DIM)].astype(F32)
        carry_ref[...] = jnp.zeros_like(carry_ref)
        acc_ref[...] = jnp.zeros_like(acc_ref)
        key = lax.broadcasted_iota(jnp.int32, (N_HEADS * rows, page), 1)
        qidx = lax.broadcasted_iota(jnp.int32, (N_HEADS * rows, page), 0) % rows
        process([knt_ref[0, pl.ds(h * HEAD_DIM, HEAD_DIM), :] for h in heads],
                [vnt_ref[0, pl.ds(h * HEAD_DIM, HEAD_DIM), :] for h in heads], 1, key < qidx)

    def head_slab(refs, h):
        return jnp.concatenate([r[h] for r in refs], axis=1).astype(BF16)

    process([head_slab(k_refs, h) for h in heads], [head_slab(v_refs, h) for h in heads], ppc, None)

    @pl.when(c == nc - 1)
    def _():
        for h in heads:
            o_ref[0, :, pl.ds(h * HEAD_DIM, HEAD_DIM)] = acc_ref[h]


def _sample_attention(page_table, q, knt, vnt, sb_bias, cache_kt, cache_vt, layer, *, ppc):
    b, rows, _ = q.shape
    n_pages = page_table.shape[1]
    page = cache_kt.shape[4]
    nchunks = n_pages // ppc
    assert nchunks * ppc == n_pages and rows % 8 == 0 and page == LANES
    body = functools.partial(_sample_attn_body, ppc=ppc, page=page)

    def page_spec(p):
        return pl.BlockSpec(
            (None, None, N_HEADS, HEAD_DIM, page),
            lambda i, j, pt: (layer, pt[i, n_pages - 1 - (j * ppc + p)], 0, 0, 0))

    per_b = lambda r, w: pl.BlockSpec((1, r, w), lambda i, j, pt: (i, 0, 0))
    grid_spec = pltpu.PrefetchScalarGridSpec(
        num_scalar_prefetch=1,
        grid=(b, nchunks),
        in_specs=[per_b(rows, ATT_WIDTH), per_b(ATT_WIDTH, page), per_b(ATT_WIDTH, page),
                  pl.BlockSpec(memory_space=pltpu.SMEM)]
                 + [page_spec(p) for p in range(ppc)] * 2,
        out_specs=per_b(rows, ATT_WIDTH),
        scratch_shapes=[pltpu.VMEM((N_HEADS, rows, HEAD_DIM), F32),
                        pltpu.VMEM((2 * page, 2 * page), BF16),
                        pltpu.VMEM((N_HEADS, rows, page), F32),
                        pltpu.VMEM((N_HEADS, rows, HEAD_DIM), F32)],
    )
    return pl.pallas_call(
        body,
        grid_spec=grid_spec,
        out_shape=jax.ShapeDtypeStruct(q.shape, F32),
        compiler_params=pltpu.CompilerParams(
            dimension_semantics=("arbitrary", "arbitrary"), vmem_limit_bytes=VMEM_LIMIT_BYTES),
        name="sample_attention",
    )(page_table, q, knt, vnt, sb_bias, *([cache_kt] * ppc), *([cache_vt] * ppc))


def _merge_body(x_ref, cact_ref, o_ref, wg_ref, wcp_ref, wap_ref, wout_ref, g_ref, b_ref,
                x1_ref, *, alpha, d, n_mix, parts):
    rows = x_ref.shape[0] // parts
    groups = [pl.ds(i * rows, rows) for i in range(parts)]
    xs = [x_ref[g, :] for g in groups]
    gates = [jnp.dot(x.astype(BF16), wg_ref[:, pl.ds(n_mix, 2 * d)], preferred_element_type=F32)
             for x in xs]
    conv_out = [jnp.dot(cact_ref[g, :], wcp_ref[...], preferred_element_type=F32) for g in groups]
    att_out = [jnp.dot(o_ref[g, :], wap_ref[...], preferred_element_type=F32) for g in groups]
    m = [jax.nn.sigmoid(gt[:, :d]) * co + jax.nn.sigmoid(gt[:, d:]) * ao
         for gt, co, ao in zip(gates, conv_out, att_out)]
    mixed = [jnp.dot(mi.astype(BF16), wout_ref[...], preferred_element_type=F32) for mi in m]
    for g, x, mx in zip(groups, xs, mixed):
        x1_ref[g, :] = _layer_norm(alpha * x + mx, g_ref[...], b_ref[...])


def _merge(x, cact, o, w_g, w_cp, w_ap, w_out, ln_g, ln_b, *, tm, alpha):
    n, d = x.shape
    assert n % tm == 0
    parts = 2 if tm % (2 * 16) == 0 else 1
    body = functools.partial(_merge_body, alpha=alpha, d=d, n_mix=w_g.shape[1] - 2 * d,
                             parts=parts)
    tile = lambda w: pl.BlockSpec((tm, w), lambda i: (i, 0))
    return pl.pallas_call(
        body,
        grid=(n // tm,),
        in_specs=[tile(d), tile(cact.shape[1]), tile(o.shape[1]),
                  _const_spec(w_g.shape), _const_spec(w_cp.shape), _const_spec(w_ap.shape),
                  _const_spec(w_out.shape), _const_spec((1, d)), _const_spec((1, d))],
        out_specs=tile(d),
        out_shape=jax.ShapeDtypeStruct((n, d), F32),
        compiler_params=pltpu.CompilerParams(
            dimension_semantics=("arbitrary",), vmem_limit_bytes=VMEM_LIMIT_BYTES),
        name="merge",
    )(x, cact, o, w_g, w_cp, w_ap, w_out, ln_g.reshape(1, d), ln_b.reshape(1, d))


def _ffn_body(x1_ref, p_ref, wup_ref, wdn_ref, wpg_ref, wple_ref, g_ref, b_ref, y_ref,
              *, alpha, hidden, chunk):
    x1 = x1_ref[...]
    x1b = x1.astype(BF16)
    ple = (jax.nn.sigmoid(jnp.dot(x1b, wpg_ref[...], preferred_element_type=F32))
           * jnp.dot(p_ref[...].astype(BF16), wple_ref[...], preferred_element_type=F32))
    acc = alpha * x1 + ple
    for c0 in range(0, hidden, chunk):
        gate = jnp.dot(x1b, wup_ref[:, pl.ds(c0, chunk)], preferred_element_type=F32)
        up = jnp.dot(x1b, wup_ref[:, pl.ds(hidden + c0, chunk)], preferred_element_type=F32)
        act = (gate * jax.nn.sigmoid(gate) * up).astype(BF16)
        acc = acc + jnp.dot(act, wdn_ref[pl.ds(c0, chunk), :], preferred_element_type=F32)
    y_ref[...] = _layer_norm(acc, g_ref[...], b_ref[...])


def _ffn(x1, p, w_up, w_dn, w_pg, w_ple, ln_g, ln_b, *, tm, alpha, chunk):
    n, d = x1.shape
    hidden = w_dn.shape[0]
    assert n % tm == 0 and hidden % chunk == 0
    body = functools.partial(_ffn_body, alpha=alpha, hidden=hidden, chunk=chunk)
    tile = lambda w: pl.BlockSpec((tm, w), lambda i: (i, 0))
    return pl.pallas_call(
        body,
        grid=(n // tm,),
        in_specs=[tile(d), tile(p.shape[1]),
                  _const_spec(w_up.shape), _const_spec(w_dn.shape), _const_spec(w_pg.shape),
                  _const_spec(w_ple.shape), _const_spec((1, d)), _const_spec((1, d))],
        out_specs=tile(d),
        out_shape=jax.ShapeDtypeStruct((n, d), F32),
        compiler_params=pltpu.CompilerParams(
            dimension_semantics=("arbitrary",), vmem_limit_bytes=VMEM_LIMIT_BYTES),
        name="ffn",
    )(x1, p, w_up, w_dn, w_pg, w_ple, ln_g.reshape(1, d), ln_b.reshape(1, d))


def _pick_tile(n, candidates):
    for c in candidates:
        if n % c == 0:
            return c
    return n


def _layer(x_p, x_s, p_p, p_s, cache_kt, cache_vt, layer, state_conv, page_table, w, alpha):
    (w_in, sb_bias, conv_w, conv_b, conv_ln_g, conv_ln_b, w_conv_proj, w_att_proj, w_out,
     ln1_g, ln1_b, w_ffn_up, w_ffn_down, w_ple_gate, w_ple, ln2_g, ln2_b) = w
    bp, tp, d = x_p.shape
    bs, ts, _ = x_s.shape
    c = conv_w.shape[1]
    w_a = w_g = w_in.astype(BF16)
    w_cp, w_ap, w_o = (a.astype(BF16) for a in (w_conv_proj, w_att_proj, w_out))
    w_up, w_dn, w_pg, w_pl = (a.astype(BF16) for a in (w_ffn_up, w_ffn_down, w_ple_gate, w_ple))
    page = cache_kt.shape[4]

    tm_p = _pick_tile(tp, (256, 128, 64, 32, 16, 8))
    cact_p, q_p, k_p, v_p, kb_p, vb_p, conv_p = _inproj_conv(
        x_p, w_a, jnp.zeros((bp, HIST_PAD, c), F32), conv_w, conv_b, conv_ln_g, conv_ln_b,
        bb=1, tm=tm_p, valid_last=tm_p)
    blk = _pick_tile(tp, (256, 128))
    o_p = _prompt_attention(sb_bias, q_p, kb_p, vb_p, blk=blk)

    ts_pad = -(-ts // 8) * 8
    x_s_pad = jnp.pad(x_s, ((0, 0), (0, ts_pad - ts), (0, 0)))
    past_pad = jnp.pad(state_conv, ((0, 0), (HIST_PAD - HIST, 0), (0, 0)))
    cact_s, q_s, k_s, v_s, kb_s, vb_s, conv_s = _inproj_conv(
        x_s_pad, w_a, past_pad, conv_w, conv_b, conv_ln_g, conv_ln_b,
        bb=_pick_tile(bs, (32, 16, 8, 4, 2, 1)), tm=ts_pad, valid_last=ts)
    knt = jnp.pad(jnp.swapaxes(kb_s, 1, 2), ((0, 0), (0, 0), (0, page - ts_pad)))
    vnt = jnp.pad(jnp.swapaxes(vb_s, 1, 2), ((0, 0), (0, 0), (0, page - ts_pad)))
    ppc = _pick_tile(page_table.shape[1], (32, 16, 8, 4, 2, 1))
    o_s = _sample_attention(page_table, q_s, knt, vnt, sb_bias, cache_kt, cache_vt, layer, ppc=ppc)

    def tail(x, cact, o, p):
        n = x.shape[0]
        tm = _pick_tile(n, (512, 256, 128, 64, 32, 16, 8))
        x1 = _merge(x, cact, o, w_g, w_cp, w_ap, w_o, ln1_g, ln1_b, tm=tm, alpha=alpha)
        return _ffn(x1, p, w_up, w_dn, w_pg, w_pl, ln2_g, ln2_b, tm=tm, alpha=alpha, chunk=256)

    y_p = tail(x_p.reshape(bp * tp, d), cact_p.reshape(bp * tp, c),
               o_p.reshape(bp * tp, ATT_WIDTH), p_p.reshape(bp * tp, -1)).reshape(bp, tp, d)
    y_s = tail(x_s.reshape(bs * ts, d), cact_s[:, :ts].reshape(bs * ts, c),
               o_s[:, :ts].reshape(bs * ts, ATT_WIDTH).astype(BF16), p_s.reshape(bs * ts, -1)).reshape(bs, ts, d)

    heads = lambda a, b, t: a[:, :t].reshape(b, t, N_HEADS, HEAD_DIM)
    return (y_p, y_s, heads(k_p, bp, tp), heads(v_p, bp, tp), conv_p,
            heads(k_s, bs, ts), heads(v_s, bs, ts), conv_s)


def kernel(x_prompt, x_sample, p_prompt, p_sample, cache_k, cache_v, state_conv, page_table, w_in, sb_bias, conv_w, conv_b, conv_ln_g, conv_ln_b, w_conv_proj, w_att_proj, w_out, ln1_g, ln1_b, w_ffn_up, w_ffn_down, w_ple_gate, w_ple, ln2_g, ln2_b):
    depth = w_in.shape[0]
    alpha = (2.0 * depth) ** 0.25
    y_p, y_s = x_prompt, x_sample
    outs = [[] for _ in range(6)]
    cache_kt = jnp.transpose(cache_k, (0, 1, 3, 4, 2))
    cache_vt = jnp.transpose(cache_v, (0, 1, 3, 4, 2))
    for i in range(depth):
        w = (w_in[i], sb_bias[i], conv_w[i], conv_b[i], conv_ln_g[i], conv_ln_b[i],
             w_conv_proj[i], w_att_proj[i], w_out[i], ln1_g[i], ln1_b[i], w_ffn_up[i],
             w_ffn_down[i], w_ple_gate[i], w_ple[i], ln2_g[i], ln2_b[i])
        y_p, y_s, k1, v1, c1, k2, v2, c2 = _layer(
            y_p, y_s, p_prompt[i], p_sample[i], cache_kt, cache_vt, i, state_conv[i],
            page_table, w, alpha)
        for lst, a in zip(outs, (k1, v1, c1, k2, v2, c2)):
            lst.append(a)
    return (y_p, y_s) + tuple(jnp.stack(lst) for lst in outs)
```

```python
import functools

import jax
import jax.numpy as jnp
from jax import lax
from jax.experimental import pallas as pl
from jax.experimental.pallas import tpu as pltpu

F32 = jnp.float32
BF16 = jnp.bfloat16

N_HEADS = 8
HEAD_DIM = 64
ATT_WIDTH = N_HEADS * HEAD_DIM
CONV_WIDTH = 31
HIST = CONV_WIDTH - 1
HIST_PAD = 32
LN_EPS = 1e-5
LANES = 128
SUBLANES = 8
HEADS_PER_LANE_TILE = LANES // HEAD_DIM
VMEM_LIMIT_BYTES = 56 * 1024 * 1024
NEG_BIG = -1e30
LOG2E = 1.4426950408889634


def _layer_norm(x, g, b):
    mu = jnp.mean(x, axis=-1, keepdims=True)
    xc = x - mu
    var = jnp.mean(xc * xc, axis=-1, keepdims=True)
    return xc * lax.rsqrt(var + LN_EPS) * g + b


def _softplus2(z2):
    neg_abs = lax.bitcast_convert_type(
        lax.bitcast_convert_type(z2, jnp.uint32) | jnp.uint32(0x80000000), F32)
    return jnp.maximum(z2, 0.0) + jnp.log2(1.0 + jnp.exp2(neg_abs))


def _split_bf16(x):
    hi = x.astype(BF16)
    lo = (x - hi.astype(F32)).astype(BF16)
    return hi, lo


def _tri_ge(n):
    j = lax.broadcasted_iota(jnp.int32, (n, n), 0)
    s = lax.broadcasted_iota(jnp.int32, (n, n), 1)
    return (j >= s).astype(BF16)


def _const_spec(shape):
    nd = len(shape)
    return pl.BlockSpec(shape, lambda *_: (0,) * nd, pipeline_mode=pl.Buffered(1))


def _conv_taps(ub, cw_ref, cb_ref, tm):
    acc = jnp.zeros((tm, ub.shape[1]), F32) + cb_ref[...]
    for rho in range(SUBLANES):
        taps = [kk for kk in range(CONV_WIDTH) if (HIST_PAD - HIST + kk) % SUBLANES == rho]
        ext = SUBLANES if rho else 0
        part = None
        for kk in taps:
            term = ub[pl.ds(HIST_PAD - HIST + kk - rho, tm + ext), :] * cw_ref[pl.ds(kk, 1), :]
            part = term if part is None else part + term
        acc = acc + part[rho:rho + tm, :]
    return acc


def _inproj_conv_body(x_ref, w_ref, past_ref, cw_ref, cb_ref, lg_ref, lb_ref,
                      cact_ref, q_ref, k_ref, v_ref, kb_ref, vb_ref, nc_ref, ubuf,
                      *, bb, tm, nt, valid_last, conv_ch):
    t = pl.program_id(1)
    d = x_ref.shape[2]

    @pl.when(t == 0)
    def _():
        ubuf[:, pl.ds(0, HIST_PAD), :] = past_ref[...]

    xb = x_ref[...].reshape(bb * tm, d).astype(BF16)
    h = jnp.dot(xb, w_ref[:, pl.ds(0, 2 * conv_ch + 3 * ATT_WIDTH)], preferred_element_type=F32)
    c = conv_ch
    u = h[:, :c] * jax.nn.sigmoid(h[:, c:2 * c])
    q = h[:, 2 * c:2 * c + ATT_WIDTH]
    k = h[:, 2 * c + ATT_WIDTH:2 * c + 2 * ATT_WIDTH]
    v = h[:, 2 * c + 2 * ATT_WIDTH:2 * c + 3 * ATT_WIDTH]
    per_seq = lambda a: a.reshape(bb, tm, a.shape[1])
    q_ref[...] = per_seq((q * (HEAD_DIM ** -0.5)).astype(BF16))
    k_ref[...] = per_seq(k)
    v_ref[...] = per_seq(v)
    kb_ref[...] = per_seq(k.astype(BF16))
    vb_ref[...] = per_seq(v.astype(BF16))

    ubuf[:, pl.ds(HIST_PAD, tm), :] = per_seq(u)
    acc = [_conv_taps(ubuf.at[s], cw_ref, cb_ref, tm) for s in range(bb)]
    acc = acc[0] if bb == 1 else jnp.concatenate(acc, axis=0)
    y = _layer_norm(acc, lg_ref[...], lb_ref[...])
    cact_ref[...] = per_seq((y * jax.nn.sigmoid(y)).astype(BF16))

    @pl.when(t == nt - 1)
    def _():
        nc_ref[...] = ubuf[:, pl.ds(HIST_PAD + valid_last - HIST, HIST), :]

    if nt > 1:
        ubuf[:, pl.ds(0, HIST_PAD), :] = ubuf[:, pl.ds(tm, HIST_PAD), :]


def _inproj_conv(x, w_a, past_pad, conv_w, conv_b, ln_g, ln_b, *, bb, tm, valid_last):
    b, t, d = x.shape
    c = conv_w.shape[1]
    nt = t // tm
    assert nt * tm == t and tm % SUBLANES == 0 and b % bb == 0
    body = functools.partial(_inproj_conv_body, bb=bb, tm=tm, nt=nt, valid_last=valid_last,
                             conv_ch=c)
    tile = lambda w: pl.BlockSpec((bb, tm, w), lambda i, j: (i, j, 0))
    per_b = lambda r, w: pl.BlockSpec((bb, r, w), lambda i, j: (i, 0, 0))
    return pl.pallas_call(
        body,
        grid=(b // bb, nt),
        in_specs=[tile(d), _const_spec(w_a.shape), per_b(HIST_PAD, c),
                  _const_spec(conv_w.shape), _const_spec((1, c)), _const_spec((1, c)),
                  _const_spec((1, c))],
        out_specs=[tile(c), tile(ATT_WIDTH), tile(ATT_WIDTH), tile(ATT_WIDTH), tile(ATT_WIDTH),
                   tile(ATT_WIDTH), per_b(HIST, c)],
        out_shape=[jax.ShapeDtypeStruct((b, t, c), BF16),
                   jax.ShapeDtypeStruct((b, t, ATT_WIDTH), BF16),
                   jax.ShapeDtypeStruct((b, t, ATT_WIDTH), F32),
                   jax.ShapeDtypeStruct((b, t, ATT_WIDTH), F32),
                   jax.ShapeDtypeStruct((b, t, ATT_WIDTH), BF16),
                   jax.ShapeDtypeStruct((b, t, ATT_WIDTH), BF16),
                   jax.ShapeDtypeStruct((b, HIST, c), F32)],
        scratch_shapes=[pltpu.VMEM((bb, HIST_PAD + tm, c), F32)],
        compiler_params=pltpu.CompilerParams(
            dimension_semantics=("arbitrary", "arbitrary"), vmem_limit_bytes=VMEM_LIMIT_BYTES),
        name="inproj_conv",
    )(x, w_a, past_pad, conv_w, conv_b.reshape(1, c), ln_g.reshape(1, c), ln_b.reshape(1, c))


def _prompt_attn_body(bias_ref, q_ref, k_ref, v_ref, o_ref, qm_ref, tri_ref, acc_ref, *, blk):
    qi = pl.program_id(1)
    row = lax.broadcasted_iota(jnp.int32, (blk, blk), 0)
    col = lax.broadcasted_iota(jnp.int32, (blk, blk), 1)
    lane = lax.broadcasted_iota(jnp.int32, (blk, LANES), 1)
    dims = (((1,), (1,)), ((), ()))
    tri_ref[...] = _tri_ge(blk)
    for h in range(N_HEADS):
        p, hh = divmod(h, HEADS_PER_LANE_TILE)
        qp = q_ref[0, :, pl.ds(p * LANES, LANES)]
        in_head = (lane >= hh * HEAD_DIM) & (lane < (hh + 1) * HEAD_DIM)
        qm_ref[h] = jnp.where(in_head, qp, jnp.zeros_like(qp))

    heads = range(N_HEADS)

    def key_blocks(lo_block, nb, carries, causal):
        keys = pl.ds(pl.multiple_of(lo_block * blk, blk), nb * blk)
        lanes = [pl.ds((h // HEADS_PER_LANE_TILE) * LANES, LANES) for h in heads]
        z2 = [(lax.dot_general(qm_ref[h], k_ref[0, keys, lanes[h]], dims,
                               preferred_element_type=F32) + bias_ref[h]) * LOG2E for h in heads]
        z2 = [[z[:, i * blk:(i + 1) * blk] for i in range(nb)] for z in z2]
        lhs = []
        for h in heads:
            for i in range(nb):
                sp = _softplus2(z2[h][i])
                if causal is not None and i == nb - 1:
                    sp = jnp.where(causal, sp, 0.0)
                lhs.append(sp.astype(BF16))
        local = [jnp.dot(x, tri_ref[...], preferred_element_type=F32) for x in lhs]
        a, new_carries = [], []
        for h in heads:
            carry = None if carries is None else carries[h]
            tiles = [None] * nb
            for i in reversed(range(nb)):
                cs = local[h * nb + i] if carry is None else local[h * nb + i] + carry
                carry = cs[:, 0:1]
                arg = z2[h][i] - cs
                if causal is not None and i == nb - 1:
                    arg = jnp.where(causal, arg, NEG_BIG)
                tiles[i] = jnp.exp2(arg).astype(BF16)
            a.append(tiles[0] if nb == 1 else jnp.concatenate(tiles, axis=1))
            new_carries.append(carry)
        for h in heads:
            pv = jnp.dot(a[h], v_ref[0, keys, lanes[h]], preferred_element_type=F32)
            if carries is None:
                acc_ref[h] = pv
            else:
                acc_ref[h] += pv
        return tuple(new_carries)

    odd = qi % 2
    carries = lax.cond(odd == 0,
                       lambda: key_blocks(qi, 1, None, col < row),
                       lambda: key_blocks(qi - 1, 2, None, col < row))
    top = qi - 1 - odd
    pairs = qi // 2
    carries = lax.cond(pairs % 2 == 1, lambda c: key_blocks(top - 1, 2, c, None), lambda c: c,
                       carries)
    top = top - 2 * (pairs % 2)
    lax.fori_loop(0, pairs // 2, lambda jj, c: key_blocks(top - 3 - 4 * jj, 4, c, None), carries)
    for p in range(N_HEADS // HEADS_PER_LANE_TILE):
        o_pair = jnp.where(lane < HEAD_DIM, acc_ref[HEADS_PER_LANE_TILE * p],
                           acc_ref[HEADS_PER_LANE_TILE * p + 1])
        o_ref[0, :, pl.ds(p * LANES, LANES)] = o_pair.astype(BF16)


def _prompt_attention(sb_bias, q, kb, vb, *, blk):
    b, t, w = q.shape
    nq = t // blk
    assert nq * blk == t and HEADS_PER_LANE_TILE == 2
    body = functools.partial(_prompt_attn_body, blk=blk)
    return pl.pallas_call(
        body,
        grid=(b, nq),
        in_specs=[pl.BlockSpec(memory_space=pltpu.SMEM),
                  pl.BlockSpec((1, blk, w), lambda i, j: (i, j, 0)),
                  pl.BlockSpec((1, t, w), lambda i, j: (i, 0, 0)),
                  pl.BlockSpec((1, t, w), lambda i, j: (i, 0, 0))],
        out_specs=pl.BlockSpec((1, blk, w), lambda i, j: (i, j, 0)),
        out_shape=jax.ShapeDtypeStruct((b, t, w), BF16),
        scratch_shapes=[pltpu.VMEM((N_HEADS, blk, LANES), BF16),
                        pltpu.VMEM((blk, blk), BF16),
                        pltpu.VMEM((N_HEADS, blk, LANES), F32)],
        compiler_params=pltpu.CompilerParams(
            dimension_semantics=("arbitrary", "arbitrary"), vmem_limit_bytes=VMEM_LIMIT_BYTES),
        name="prompt_attention",
    )(sb_bias, q, kb, vb)


def _sample_attn_body(pt_ref, q_ref, knt_ref, vnt_ref, bias_ref, *rest, ppc, page):
    k_refs = rest[:ppc]
    v_refs = rest[ppc:2 * ppc]
    o_ref = rest[2 * ppc]
    qh_ref, tri_ref, carry_ref, acc_ref = rest[2 * ppc + 1:]
    del pt_ref
    c = pl.program_id(1)
    nc = pl.num_programs(1)
    rows = q_ref.shape[1]
    dims_nt = (((1,), (1,)), ((), ()))
    heads = range(N_HEADS)

    def process(kt_heads, vt_heads, n, valid):
        z2 = [(jnp.dot(qh_ref[h].astype(BF16), kt_heads[h], preferred_element_type=F32)
               + bias_ref[h]) * LOG2E for h in heads]
        zs = jnp.concatenate([z2[h][:, p * page:(p + 1) * page] for h in heads for p in range(n)],
                             axis=0)
        sp = _softplus2(zs)
        if valid is not None:
            sp = jnp.where(valid, sp, 0.0)
        r = jnp.dot(jnp.concatenate(_split_bf16(sp), axis=1), tri_ref[...],
                    preferred_element_type=F32)
        local, total = r[:, :page], r[:, page:]
        a = []
        for h in heads:
            carry = carry_ref[h]
            tiles = []
            for p in range(n):
                rs = slice((h * n + p) * rows, (h * n + p + 1) * rows)
                arg = zs[rs] - (local[rs] + carry)
                if valid is not None:
                    arg = jnp.where(valid[rs], arg, NEG_BIG)
                tiles.append(jnp.exp2(arg))
                carry = carry + total[rs]
            carry_ref[h] = carry
            a.append(jnp.concatenate(tiles, axis=1).astype(BF16))
        for h in heads:
            acc_ref[h] += lax.dot_general(a[h], vt_heads[h], dims_nt, preferred_element_type=F32)

    @pl.when(c == 0)
    def _():
        tri = _tri_ge(page)
        both = jnp.concatenate([tri, jnp.ones_like(tri)], axis=1)
        tri_ref[pl.ds(0, page), :] = both
        tri_ref[pl.ds(page, page), :] = both
        for h in heads:
            qh_ref[h] = q_ref[0, :, pl.ds(h * HEAD_DIM, HEAD_DIM)].astype(F32)
        carry_ref[...] = jnp.zeros_like(carry_ref)
        acc_ref[...] = jnp.zeros_like(acc_ref)
        key = lax.broadcasted_iota(jnp.int32, (N_HEADS * rows, page), 1)
        qidx = lax.broadcasted_iota(jnp.int32, (N_HEADS * rows, page), 0) % rows
        process([knt_ref[0, pl.ds(h * HEAD_DIM, HEAD_DIM), :] for h in heads],
                [vnt_ref[0, pl.ds(h * HEAD_DIM, HEAD_DIM), :] for h in heads], 1, key < qidx)

    def head_slab(refs, h):
        return jnp.concatenate([r[h] for r in refs], axis=1).astype(BF16)

    process([head_slab(k_refs, h) for h in heads], [head_slab(v_refs, h) for h in heads], ppc, None)

    @pl.when(c == nc - 1)
    def _():
        for h in heads:
            o_ref[0, :, pl.ds(h * HEAD_DIM, HEAD_DIM)] = acc_ref[h]


def _sample_attention(page_table, q, knt, vnt, sb_bias, cache_kt, cache_vt, layer, *, ppc):
    b, rows, _ = q.shape
    n_pages = page_table.shape[1]
    page = cache_kt.shape[4]
    nchunks = n_pages // ppc
    assert nchunks * ppc == n_pages and rows % 8 == 0 and page == LANES
    body = functools.partial(_sample_attn_body, ppc=ppc, page=page)

    def page_spec(p):
        return pl.BlockSpec(
            (None, None, N_HEADS, HEAD_DIM, page),
            lambda i, j, pt: (layer, pt[i, n_pages - 1 - (j * ppc + p)], 0, 0, 0))

    per_b = lambda r, w: pl.BlockSpec((1, r, w), lambda i, j, pt: (i, 0, 0))
    grid_spec = pltpu.PrefetchScalarGridSpec(
        num_scalar_prefetch=1,
        grid=(b, nchunks),
        in_specs=[per_b(rows, ATT_WIDTH), per_b(ATT_WIDTH, page), per_b(ATT_WIDTH, page),
                  pl.BlockSpec(memory_space=pltpu.SMEM)]
                 + [page_spec(p) for p in range(ppc)] * 2,
        out_specs=per_b(rows, ATT_WIDTH),
        scratch_shapes=[pltpu.VMEM((N_HEADS, rows, HEAD_DIM), F32),
                        pltpu.VMEM((2 * page, 2 * page), BF16),
                        pltpu.VMEM((N_HEADS, rows, page), F32),
                        pltpu.VMEM((N_HEADS, rows, HEAD_DIM), F32)],
    )
    return pl.pallas_call(
        body,
        grid_spec=grid_spec,
        out_shape=jax.ShapeDtypeStruct(q.shape, F32),
        compiler_params=pltpu.CompilerParams(
            dimension_semantics=("arbitrary", "arbitrary"), vmem_limit_bytes=VMEM_LIMIT_BYTES),
        name="sample_attention",
    )(page_table, q, knt, vnt, sb_bias, *([cache_kt] * ppc), *([cache_vt] * ppc))


def _merge_body(x_ref, cact_ref, o_ref, wg_ref, wcp_ref, wap_ref, wout_ref, g_ref, b_ref,
                x1_ref, *, alpha, d, n_mix, parts):
    rows = x_ref.shape[0] // parts
    groups = [pl.ds(i * rows, rows) for i in range(parts)]
    xs = [x_ref[g, :] for g in groups]
    gates = [jnp.dot(x.astype(BF16), wg_ref[:, pl.ds(n_mix, 2 * d)], preferred_element_type=F32)
             for x in xs]
    conv_out = [jnp.dot(cact_ref[g, :], wcp_ref[...], preferred_element_type=F32) for g in groups]
    att_out = [jnp.dot(o_ref[g, :], wap_ref[...], preferred_element_type=F32) for g in groups]
    m = [jax.nn.sigmoid(gt[:, :d]) * co + jax.nn.sigmoid(gt[:, d:]) * ao
         for gt, co, ao in zip(gates, conv_out, att_out)]
    mixed = [jnp.dot(mi.astype(BF16), wout_ref[...], preferred_element_type=F32) for mi in m]
    for g, x, mx in zip(groups, xs, mixed):
        x1_ref[g, :] = _layer_norm(alpha * x + mx, g_ref[...], b_ref[...])


def _merge(x, cact, o, w_g, w_cp, w_ap, w_out, ln_g, ln_b, *, tm, alpha):
    n, d = x.shape
    assert n % tm == 0
    parts = 2 if tm % (2 * 16) == 0 else 1
    body = functools.partial(_merge_body, alpha=alpha, d=d, n_mix=w_g.shape[1] - 2 * d,
                             parts=parts)
    tile = lambda w: pl.BlockSpec((tm, w), lambda i: (i, 0))
    return pl.pallas_call(
        body,
        grid=(n // tm,),
        in_specs=[tile(d), tile(cact.shape[1]), tile(o.shape[1]),
                  _const_spec(w_g.shape), _const_spec(w_cp.shape), _const_spec(w_ap.shape),
                  _const_spec(w_out.shape), _const_spec((1, d)), _const_spec((1, d))],
        out_specs=tile(d),
        out_shape=jax.ShapeDtypeStruct((n, d), F32),
        compiler_params=pltpu.CompilerParams(
            dimension_semantics=("arbitrary",), vmem_limit_bytes=VMEM_LIMIT_BYTES),
        name="merge",
    )(x, cact, o, w_g, w_cp, w_ap, w_out, ln_g.reshape(1, d), ln_b.reshape(1, d))


def _ffn_body(x1_ref, p_ref, wup_ref, wdn_ref, wpg_ref, wple_ref, g_ref, b_ref, y_ref,
              *, alpha, hidden, chunk):
    x1 = x1_ref[...]
    x1b = x1.astype(BF16)
    ple = (jax.nn.sigmoid(jnp.dot(x1b, wpg_ref[...], preferred_element_type=F32))
           * jnp.dot(p_ref[...].astype(BF16), wple_ref[...], preferred_element_type=F32))
    acc = alpha * x1 + ple
    for c0 in range(0, hidden, chunk):
        gate = jnp.dot(x1b, wup_ref[:, pl.ds(c0, chunk)], preferred_element_type=F32)
        up = jnp.dot(x1b, wup_ref[:, pl.ds(hidden + c0, chunk)], preferred_element_type=F32)
        act = (gate * jax.nn.sigmoid(gate) * up).astype(BF16)
        acc = acc + jnp.dot(act, wdn_ref[pl.ds(c0, chunk), :], preferred_element_type=F32)
    y_ref[...] = _layer_norm(acc, g_ref[...], b_ref[...])


def _ffn(x1, p, w_up, w_dn, w_pg, w_ple, ln_g, ln_b, *, tm, alpha, chunk):
    n, d = x1.shape
    hidden = w_dn.shape[0]
    assert n % tm == 0 and hidden % chunk == 0
    body = functools.partial(_ffn_body, alpha=alpha, hidden=hidden, chunk=chunk)
    tile = lambda w: pl.BlockSpec((tm, w), lambda i: (i, 0))
    return pl.pallas_call(
        body,
        grid=(n // tm,),
        in_specs=[tile(d), tile(p.shape[1]),
                  _const_spec(w_up.shape), _const_spec(w_dn.shape), _const_spec(w_pg.shape),
                  _const_spec(w_ple.shape), _const_spec((1, d)), _const_spec((1, d))],
        out_specs=tile(d),
        out_shape=jax.ShapeDtypeStruct((n, d), F32),
        compiler_params=pltpu.CompilerParams(
            dimension_semantics=("arbitrary",), vmem_limit_bytes=VMEM_LIMIT_BYTES),
        name="ffn",
    )(x1, p, w_up, w_dn, w_pg, w_ple, ln_g.reshape(1, d), ln_b.reshape(1, d))


def _pick_tile(n, candidates):
    for c in candidates:
        if n % c == 0:
            return c
    return n


def _layer(x_p, x_s, p_p, p_s, cache_kt, cache_vt, layer, state_conv, page_table, w, alpha):
    (w_in, sb_bias, conv_w, conv_b, conv_ln_g, conv_ln_b, w_conv_proj, w_att_proj, w_out,
     ln1_g, ln1_b, w_ffn_up, w_ffn_down, w_ple_gate, w_ple, ln2_g, ln2_b) = w
    bp, tp, d = x_p.shape
    bs, ts, _ = x_s.shape
    c = conv_w.shape[1]
    w_a = w_g = w_in.astype(BF16)
    w_cp, w_ap, w_o = (a.astype(BF16) for a in (w_conv_proj, w_att_proj, w_out))
    w_up, w_dn, w_pg, w_pl = (a.astype(BF16) for a in (w_ffn_up, w_ffn_down, w_ple_gate, w_ple))
    page = cache_kt.shape[4]

    tm_p = _pick_tile(tp, (256, 128, 64, 32, 16, 8))
    cact_p, q_p, k_p, v_p, kb_p, vb_p, conv_p = _inproj_conv(
        x_p, w_a, jnp.zeros((bp, HIST_PAD, c), F32), conv_w, conv_b, conv_ln_g, conv_ln_b,
        bb=1, tm=tm_p, valid_last=tm_p)
    blk = _pick_tile(tp, (256, 128))
    o_p = _prompt_attention(sb_bias, q_p, kb_p, vb_p, blk=blk)

    ts_pad = -(-ts // 8) * 8
    x_s_pad = jnp.pad(x_s, ((0, 0), (0, ts_pad - ts), (0, 0)))
    past_pad = jnp.pad(state_conv, ((0, 0), (HIST_PAD - HIST, 0), (0, 0)))
    cact_s, q_s, k_s, v_s, kb_s, vb_s, conv_s = _inproj_conv(
        x_s_pad, w_a, past_pad, conv_w, conv_b, conv_ln_g, conv_ln_b,
        bb=_pick_tile(bs, (32, 16, 8, 4, 2, 1)), tm=ts_pad, valid_last=ts)
    knt = jnp.pad(jnp.swapaxes(kb_s, 1, 2), ((0, 0), (0, 0), (0, page - ts_pad)))
    vnt = jnp.pad(jnp.swapaxes(vb_s, 1, 2), ((0, 0), (0, 0), (0, page - ts_pad)))
    ppc = _pick_tile(page_table.shape[1], (32, 16, 8, 4, 2, 1))
    o_s = _sample_attention(page_table, q_s, knt, vnt, sb_bias, cache_kt, cache_vt, layer, ppc=ppc)

    def tail(x, cact, o, p):
        n = x.shape[0]
        tm = _pick_tile(n, (512, 256, 128, 64, 32, 16, 8))
        x1 = _merge(x, cact, o, w_g, w_cp, w_ap, w_o, ln1_g, ln1_b, tm=tm, alpha=alpha)
        return _ffn(x1, p, w_up, w_dn, w_pg, w_pl, ln2_g, ln2_b, tm=tm, alpha=alpha, chunk=256)

    y_p = tail(x_p.reshape(bp * tp, d), cact_p.reshape(bp * tp, c),
               o_p.reshape(bp * tp, ATT_WIDTH), p_p.reshape(bp * tp, -1)).reshape(bp, tp, d)
    y_s = tail(x_s.reshape(bs * ts, d), cact_s[:, :ts].reshape(bs * ts, c),
               o_s[:, :ts].reshape(bs * ts, ATT_WIDTH).astype(BF16), p_s.reshape(bs * ts, -1)).reshape(bs, ts, d)

    heads = lambda a, b, t: a[:, :t].reshape(b, t, N_HEADS, HEAD_DIM)
    return (y_p, y_s, heads(k_p, bp, tp), heads(v_p, bp, tp), conv_p,
            heads(k_s, bs, ts), heads(v_s, bs, ts), conv_s)


def kernel(x_prompt, x_sample, p_prompt, p_sample, cache_k, cache_v, state_conv, page_table, w_in, sb_bias, conv_w, conv_b, conv_ln_g, conv_ln_b, w_conv_proj, w_att_proj, w_out, ln1_g, ln1_b, w_ffn_up, w_ffn_down, w_ple_gate, w_ple, ln2_g, ln2_b):
    depth = w_in.shape[0]
    alpha = (2.0 * depth) ** 0.25
    y_p, y_s = x_prompt, x_sample
    outs = [[] for _ in range(6)]
    cache_kt = jnp.transpose(cache_k, (0, 1, 3, 4, 2))
    cache_vt = jnp.transpose(cache_v, (0, 1, 3, 4, 2))
    for i in range(depth):
        w = (w_in[i], sb_bias[i], conv_w[i], conv_b[i], conv_ln_g[i], conv_ln_b[i],
             w_conv_proj[i], w_att_proj[i], w_out[i], ln1_g[i], ln1_b[i], w_ffn_up[i],
             w_ffn_down[i], w_ple_gate[i], w_ple[i], ln2_g[i], ln2_b[i])
        y_p, y_s, k1, v1, c1, k2, v2, c2 = _layer(
            y_p, y_s, p_prompt[i], p_sample[i], cache_kt, cache_vt, i, state_conv[i],
            page_table, w, alpha)
        for lst, a in zip(outs, (k1, v1, c1, k2, v2, c2)):
            lst.append(a)
    return (y_p, y_s) + tuple(jnp.stack(lst) for lst in outs)
```

```python
import functools

import jax
import jax.numpy as jnp
from jax import lax
from jax.experimental import pallas as pl
from jax.experimental.pallas import tpu as pltpu

F32 = jnp.float32
BF16 = jnp.bfloat16

N_HEADS = 8
HEAD_DIM = 64
ATT_WIDTH = N_HEADS * HEAD_DIM
CONV_WIDTH = 31
HIST = CONV_WIDTH - 1
HIST_PAD = 32
LN_EPS = 1e-5
LANES = 128
SUBLANES = 8
HEADS_PER_LANE_TILE = LANES // HEAD_DIM
VMEM_LIMIT_BYTES = 56 * 1024 * 1024
NEG_BIG = -1e30
LOG2E = 1.4426950408889634


def _layer_norm(x, g, b):
    mu = jnp.mean(x, axis=-1, keepdims=True)
    xc = x - mu
    var = jnp.mean(xc * xc, axis=-1, keepdims=True)
    return xc * lax.rsqrt(var + LN_EPS) * g + b


def _softplus2(z2):
    neg_abs = lax.bitcast_convert_type(
        lax.bitcast_convert_type(z2, jnp.uint32) | jnp.uint32(0x80000000), F32)
    return jnp.maximum(z2, 0.0) + jnp.log2(1.0 + jnp.exp2(neg_abs))


def _split_bf16(x):
    hi = x.astype(BF16)
    lo = (x - hi.astype(F32)).astype(BF16)
    return hi, lo


def _tri_ge(n):
    j = lax.broadcasted_iota(jnp.int32, (n, n), 0)
    s = lax.broadcasted_iota(jnp.int32, (n, n), 1)
    return (j >= s).astype(BF16)


def _const_spec(shape):
    nd = len(shape)
    return pl.BlockSpec(shape, lambda *_: (0,) * nd, pipeline_mode=pl.Buffered(1))


def _conv_taps(ub, cw_ref, cb_ref, tm):
    acc = jnp.zeros((tm, ub.shape[1]), F32) + cb_ref[...]
    for rho in range(SUBLANES):
        taps = [kk for kk in range(CONV_WIDTH) if (HIST_PAD - HIST + kk) % SUBLANES == rho]
        ext = SUBLANES if rho else 0
        part = None
        for kk in taps:
            term = ub[pl.ds(HIST_PAD - HIST + kk - rho, tm + ext), :] * cw_ref[pl.ds(kk, 1), :]
            part = term if part is None else part + term
        acc = acc + part[rho:rho + tm, :]
    return acc


def _inproj_conv_body(x_ref, w_ref, past_ref, cw_ref, cb_ref, lg_ref, lb_ref,
                      cact_ref, q_ref, k_ref, v_ref, kb_ref, vb_ref, nc_ref, ubuf,
                      *, bb, tm, nt, valid_last, conv_ch):
    t = pl.program_id(1)
    d = x_ref.shape[2]

    @pl.when(t == 0)
    def _():
        ubuf[:, pl.ds(0, HIST_PAD), :] = past_ref[...]

    xb = x_ref[...].reshape(bb * tm, d).astype(BF16)
    h = jnp.dot(xb, w_ref[:, pl.ds(0, 2 * conv_ch + 3 * ATT_WIDTH)], preferred_element_type=F32)
    c = conv_ch
    u = h[:, :c] * jax.nn.sigmoid(h[:, c:2 * c])
    q = h[:, 2 * c:2 * c + ATT_WIDTH]
    k = h[:, 2 * c + ATT_WIDTH:2 * c + 2 * ATT_WIDTH]
    v = h[:, 2 * c + 2 * ATT_WIDTH:2 * c + 3 * ATT_WIDTH]
    per_seq = lambda a: a.reshape(bb, tm, a.shape[1])
    q_ref[...] = per_seq((q * (HEAD_DIM ** -0.5)).astype(BF16))
    k_ref[...] = per_seq(k)
    v_ref[...] = per_seq(v)
    kb_ref[...] = per_seq(k.astype(BF16))
    vb_ref[...] = per_seq(v.astype(BF16))

    ubuf[:, pl.ds(HIST_PAD, tm), :] = per_seq(u)
    acc = [_conv_taps(ubuf.at[s], cw_ref, cb_ref, tm) for s in range(bb)]
    acc = acc[0] if bb == 1 else jnp.concatenate(acc, axis=0)
    y = _layer_norm(acc, lg_ref[...], lb_ref[...])
    cact_ref[...] = per_seq((y * jax.nn.sigmoid(y)).astype(BF16))

    @pl.when(t == nt - 1)
    def _():
        nc_ref[...] = ubuf[:, pl.ds(HIST_PAD + valid_last - HIST, HIST), :]

    if nt > 1:
        ubuf[:, pl.ds(0, HIST_PAD), :] = ubuf[:, pl.ds(tm, HIST_PAD), :]


def _inproj_conv(x, w_a, past_pad, conv_w, conv_b, ln_g, ln_b, *, bb, tm, valid_last):
    b, t, d = x.shape
    c = conv_w.shape[1]
    nt = t // tm
    assert nt * tm == t and tm % SUBLANES == 0 and b % bb == 0
    body = functools.partial(_inproj_conv_body, bb=bb, tm=tm, nt=nt, valid_last=valid_last,
                             conv_ch=c)
    tile = lambda w: pl.BlockSpec((bb, tm, w), lambda i, j: (i, j, 0))
    per_b = lambda r, w: pl.BlockSpec((bb, r, w), lambda i, j: (i, 0, 0))
    return pl.pallas_call(
        body,
        grid=(b // bb, nt),
        in_specs=[tile(d), _const_spec(w_a.shape), per_b(HIST_PAD, c),
                  _const_spec(conv_w.shape), _const_spec((1, c)), _const_spec((1, c)),
                  _const_spec((1, c))],
        out_specs=[tile(c), tile(ATT_WIDTH), tile(ATT_WIDTH), tile(ATT_WIDTH), tile(ATT_WIDTH),
                   tile(ATT_WIDTH), per_b(HIST, c)],
        out_shape=[jax.ShapeDtypeStruct((b, t, c), BF16),
                   jax.ShapeDtypeStruct((b, t, ATT_WIDTH), BF16),
                   jax.ShapeDtypeStruct((b, t, ATT_WIDTH), F32),
                   jax.ShapeDtypeStruct((b, t, ATT_WIDTH), F32),
                   jax.ShapeDtypeStruct((b, t, ATT_WIDTH), BF16),
                   jax.ShapeDtypeStruct((b, t, ATT_WIDTH), BF16),
                   jax.ShapeDtypeStruct((b, HIST, c), F32)],
        scratch_shapes=[pltpu.VMEM((bb, HIST_PAD + tm, c), F32)],
        compiler_params=pltpu.CompilerParams(
            dimension_semantics=("arbitrary", "arbitrary"), vmem_limit_bytes=VMEM_LIMIT_BYTES),
        name="inproj_conv",
    )(x, w_a, past_pad, conv_w, conv_b.reshape(1, c), ln_g.reshape(1, c), ln_b.reshape(1, c))


def _prompt_attn_body(bias_ref, q_ref, k_ref, v_ref, o_ref, qm_ref, tri_ref, acc_ref, *, blk):
    qi = pl.program_id(1)
    row = lax.broadcasted_iota(jnp.int32, (blk, blk), 0)
    col = lax.broadcasted_iota(jnp.int32, (blk, blk), 1)
    lane = lax.broadcasted_iota(jnp.int32, (blk, LANES), 1)
    dims = (((1,), (1,)), ((), ()))
    tri_ref[...] = _tri_ge(blk)
    for h in range(N_HEADS):
        p, hh = divmod(h, HEADS_PER_LANE_TILE)
        qp = q_ref[0, :, pl.ds(p * LANES, LANES)]
        in_head = (lane >= hh * HEAD_DIM) & (lane < (hh + 1) * HEAD_DIM)
        qm_ref[h] = jnp.where(in_head, qp, jnp.zeros_like(qp))

    heads = range(N_HEADS)

    def key_blocks(lo_block, nb, carries, causal):
        keys = pl.ds(pl.multiple_of(lo_block * blk, blk), nb * blk)
        lanes = [pl.ds((h // HEADS_PER_LANE_TILE) * LANES, LANES) for h in heads]
        z2 = [(lax.dot_general(qm_ref[h], k_ref[0, keys, lanes[h]], dims,
                               preferred_element_type=F32) + bias_ref[h]) * LOG2E for h in heads]
        z2 = [[z[:, i * blk:(i + 1) * blk] for i in range(nb)] for z in z2]
        lhs = []
        for h in heads:
            for i in range(nb):
                sp = _softplus2(z2[h][i])
                if causal is not None and i == nb - 1:
                    sp = jnp.where(causal, sp, 0.0)
                lhs.append(sp.astype(BF16))
        local = [jnp.dot(x, tri_ref[...], preferred_element_type=F32) for x in lhs]
        a, new_carries = [], []
        for h in heads:
            carry = None if carries is None else carries[h]
            tiles = [None] * nb
            for i in reversed(range(nb)):
                cs = local[h * nb + i] if carry is None else local[h * nb + i] + carry
                carry = cs[:, 0:1]
                arg = z2[h][i] - cs
                if causal is not None and i == nb - 1:
                    arg = jnp.where(causal, arg, NEG_BIG)
                tiles[i] = jnp.exp2(arg).astype(BF16)
            a.append(tiles[0] if nb == 1 else jnp.concatenate(tiles, axis=1))
            new_carries.append(carry)
        for h in heads:
            pv = jnp.dot(a[h], v_ref[0, keys, lanes[h]], preferred_element_type=F32)
            if carries is None:
                acc_ref[h] = pv
            else:
                acc_ref[h] += pv
        return tuple(new_carries)

    odd = qi % 2
    carries = lax.cond(odd == 0,
                       lambda: key_blocks(qi, 1, None, col < row),
                       lambda: key_blocks(qi - 1, 2, None, col < row))
    top = qi - 1 - odd
    lax.fori_loop(0, qi // 2, lambda jj, c: key_blocks(top - 1 - 2 * jj, 2, c, None), carries)
    for p in range(N_HEADS // HEADS_PER_LANE_TILE):
        o_pair = jnp.where(lane < HEAD_DIM, acc_ref[HEADS_PER_LANE_TILE * p],
                           acc_ref[HEADS_PER_LANE_TILE * p + 1])
        o_ref[0, :, pl.ds(p * LANES, LANES)] = o_pair.astype(BF16)


def _prompt_attention(sb_bias, q, kb, vb, *, blk):
    b, t, w = q.shape
    nq = t // blk
    assert nq * blk == t and HEADS_PER_LANE_TILE == 2
    body = functools.partial(_prompt_attn_body, blk=blk)
    return pl.pallas_call(
        body,
        grid=(b, nq),
        in_specs=[pl.BlockSpec(memory_space=pltpu.SMEM),
                  pl.BlockSpec((1, blk, w), lambda i, j: (i, j, 0)),
                  pl.BlockSpec((1, t, w), lambda i, j: (i, 0, 0)),
                  pl.BlockSpec((1, t, w), lambda i, j: (i, 0, 0))],
        out_specs=pl.BlockSpec((1, blk, w), lambda i, j: (i, j, 0)),
        out_shape=jax.ShapeDtypeStruct((b, t, w), BF16),
        scratch_shapes=[pltpu.VMEM((N_HEADS, blk, LANES), BF16),
                        pltpu.VMEM((blk, blk), BF16),
                        pltpu.VMEM((N_HEADS, blk, LANES), F32)],
        compiler_params=pltpu.CompilerParams(
            dimension_semantics=("arbitrary", "arbitrary"), vmem_limit_bytes=VMEM_LIMIT_BYTES),
        name="prompt_attention",
    )(sb_bias, q, kb, vb)


def _sample_attn_body(pt_ref, q_ref, knt_ref, vnt_ref, bias_ref, *rest, ppc, page):
    k_refs = rest[:ppc]
    v_refs = rest[ppc:2 * ppc]
    o_ref = rest[2 * ppc]
    qh_ref, tri_ref, carry_ref, acc_ref = rest[2 * ppc + 1:]
    del pt_ref
    c = pl.program_id(1)
    nc = pl.num_programs(1)
    rows = q_ref.shape[1]
    dims_nt = (((1,), (1,)), ((), ()))
    heads = range(N_HEADS)

    def process(kt_heads, vt_heads, n, valid):
        z2 = [(jnp.dot(qh_ref[h].astype(BF16), kt_heads[h], preferred_element_type=F32)
               + bias_ref[h]) * LOG2E for h in heads]
        zs = jnp.concatenate([z2[h][:, p * page:(p + 1) * page] for h in heads for p in range(n)],
                             axis=0)
        sp = _softplus2(zs)
        if valid is not None:
            sp = jnp.where(valid, sp, 0.0)
        r = jnp.dot(jnp.concatenate(_split_bf16(sp), axis=1), tri_ref[...],
                    preferred_element_type=F32)
        local, total = r[:, :page], r[:, page:]
        a = []
        for h in heads:
            carry = carry_ref[h]
            tiles = []
            for p in range(n):
                rs = slice((h * n + p) * rows, (h * n + p + 1) * rows)
                arg = zs[rs] - (local[rs] + carry)
                if valid is not None:
                    arg = jnp.where(valid[rs], arg, NEG_BIG)
                tiles.append(jnp.exp2(arg))
                carry = carry + total[rs]
            carry_ref[h] = carry
            a.append(jnp.concatenate(tiles, axis=1).astype(BF16))
        for h in heads:
            acc_ref[h] += lax.dot_general(a[h], vt_heads[h], dims_nt, preferred_element_type=F32)

    @pl.when(c == 0)
    def _():
        tri = _tri_ge(page)
        both = jnp.concatenate([tri, jnp.ones_like(tri)], axis=1)
        tri_ref[pl.ds(0, page), :] = both
        tri_ref[pl.ds(page, page), :] = both
        for h in heads:
            qh_ref[h] = q_ref[0, :, pl.ds(h * HEAD_DIM, HEAD_DIM)].astype(F32)
        carry_ref[...] = jnp.zeros_like(carry_ref)
        acc_ref[...] = jnp.zeros_like(acc_ref)
        key = lax.broadcasted_iota(jnp.int32, (N_HEADS * rows, page), 1)
        qidx = lax.broadcasted_iota(jnp.int32, (N_HEADS * rows, page), 0) % rows
        process([knt_ref[0, pl.ds(h * HEAD_DIM, HEAD_DIM), :] for h in heads],
                [vnt_ref[0, pl.ds(h * HEAD_DIM, HEAD_DIM), :] for h in heads], 1, key < qidx)

    def head_slab(refs, h):
        return jnp.concatenate([r[h] for r in refs], axis=1).astype(BF16)

    process([head_slab(k_refs, h) for h in heads], [head_slab(v_refs, h) for h in heads], ppc, None)

    @pl.when(c == nc - 1)
    def _():
        for h in heads:
            o_ref[0, :, pl.ds(h * HEAD_DIM, HEAD_DIM)] = acc_ref[h]


def _sample_attention(page_table, q, knt, vnt, sb_bias, cache_kt, cache_vt, layer, *, ppc):
    b, rows, _ = q.shape
    n_pages = page_table.shape[1]
    page = cache_kt.shape[4]
    nchunks = n_pages // ppc
    assert nchunks * ppc == n_pages and rows % 8 == 0 and page == LANES
    body = functools.partial(_sample_attn_body, ppc=ppc, page=page)

    def page_spec(p):
        return pl.BlockSpec(
            (None, None, N_HEADS, HEAD_DIM, page),
            lambda i, j, pt: (layer, pt[i, n_pages - 1 - (j * ppc + p)], 0, 0, 0))

    per_b = lambda r, w: pl.BlockSpec((1, r, w), lambda i, j, pt: (i, 0, 0))
    grid_spec = pltpu.PrefetchScalarGridSpec(
        num_scalar_prefetch=1,
        grid=(b, nchunks),
        in_specs=[per_b(rows, ATT_WIDTH), per_b(ATT_WIDTH, page), per_b(ATT_WIDTH, page),
                  pl.BlockSpec(memory_space=pltpu.SMEM)]
                 + [page_spec(p) for p in range(ppc)] * 2,
        out_specs=per_b(rows, ATT_WIDTH),
        scratch_shapes=[pltpu.VMEM((N_HEADS, rows, HEAD_DIM), F32),
                        pltpu.VMEM((2 * page, 2 * page), BF16),
                        pltpu.VMEM((N_HEADS, rows, page), F32),
                        pltpu.VMEM((N_HEADS, rows, HEAD_DIM), F32)],
    )
    return pl.pallas_call(
        body,
        grid_spec=grid_spec,
        out_shape=jax.ShapeDtypeStruct(q.shape, F32),
        compiler_params=pltpu.CompilerParams(
            dimension_semantics=("arbitrary", "arbitrary"), vmem_limit_bytes=VMEM_LIMIT_BYTES),
        name="sample_attention",
    )(page_table, q, knt, vnt, sb_bias, *([cache_kt] * ppc), *([cache_vt] * ppc))


def _merge_body(x_ref, cact_ref, o_ref, wg_ref, wcp_ref, wap_ref, wout_ref, g_ref, b_ref,
                x1_ref, *, alpha, d, n_mix, parts):
    rows = x_ref.shape[0] // parts
    groups = [pl.ds(i * rows, rows) for i in range(parts)]
    xs = [x_ref[g, :] for g in groups]
    gates = [jnp.dot(x.astype(BF16), wg_ref[:, pl.ds(n_mix, 2 * d)], preferred_element_type=F32)
             for x in xs]
    conv_out = [jnp.dot(cact_ref[g, :], wcp_ref[...], preferred_element_type=F32) for g in groups]
    att_out = [jnp.dot(o_ref[g, :], wap_ref[...], preferred_element_type=F32) for g in groups]
    m = [jax.nn.sigmoid(gt[:, :d]) * co + jax.nn.sigmoid(gt[:, d:]) * ao
         for gt, co, ao in zip(gates, conv_out, att_out)]
    mixed = [jnp.dot(mi.astype(BF16), wout_ref[...], preferred_element_type=F32) for mi in m]
    for g, x, mx in zip(groups, xs, mixed):
        x1_ref[g, :] = _layer_norm(alpha * x + mx, g_ref[...], b_ref[...])


def _merge(x, cact, o, w_g, w_cp, w_ap, w_out, ln_g, ln_b, *, tm, alpha):
    n, d = x.shape
    assert n % tm == 0
    parts = 2 if tm % (2 * 16) == 0 else 1
    body = functools.partial(_merge_body, alpha=alpha, d=d, n_mix=w_g.shape[1] - 2 * d,
                             parts=parts)
    tile = lambda w: pl.BlockSpec((tm, w), lambda i: (i, 0))
    return pl.pallas_call(
        body,
        grid=(n // tm,),
        in_specs=[tile(d), tile(cact.shape[1]), tile(o.shape[1]),
                  _const_spec(w_g.shape), _const_spec(w_cp.shape), _const_spec(w_ap.shape),
                  _const_spec(w_out.shape), _const_spec((1, d)), _const_spec((1, d))],
        out_specs=tile(d),
        out_shape=jax.ShapeDtypeStruct((n, d), F32),
        compiler_params=pltpu.CompilerParams(
            dimension_semantics=("arbitrary",), vmem_limit_bytes=VMEM_LIMIT_BYTES),
        name="merge",
    )(x, cact, o, w_g, w_cp, w_ap, w_out, ln_g.reshape(1, d), ln_b.reshape(1, d))


def _ffn_body(x1_ref, p_ref, wup_ref, wdn_ref, wpg_ref, wple_ref, g_ref, b_ref, y_ref,
              *, alpha, hidden, chunk):
    x1 = x1_ref[...]
    x1b = x1.astype(BF16)
    ple = (jax.nn.sigmoid(jnp.dot(x1b, wpg_ref[...], preferred_element_type=F32))
           * jnp.dot(p_ref[...].astype(BF16), wple_ref[...], preferred_element_type=F32))
    acc = alpha * x1 + ple
    for c0 in range(0, hidden, chunk):
        gate = jnp.dot(x1b, wup_ref[:, pl.ds(c0, chunk)], preferred_element_type=F32)
        up = jnp.dot(x1b, wup_ref[:, pl.ds(hidden + c0, chunk)], preferred_element_type=F32)
        act = (gate * jax.nn.sigmoid(gate) * up).astype(BF16)
        acc = acc + jnp.dot(act, wdn_ref[pl.ds(c0, chunk), :], preferred_element_type=F32)
    y_ref[...] = _layer_norm(acc, g_ref[...], b_ref[...])


def _ffn(x1, p, w_up, w_dn, w_pg, w_ple, ln_g, ln_b, *, tm, alpha, chunk):
    n, d = x1.shape
    hidden = w_dn.shape[0]
    assert n % tm == 0 and hidden % chunk == 0
    body = functools.partial(_ffn_body, alpha=alpha, hidden=hidden, chunk=chunk)
    tile = lambda w: pl.BlockSpec((tm, w), lambda i: (i, 0))
    return pl.pallas_call(
        body,
        grid=(n // tm,),
        in_specs=[tile(d), tile(p.shape[1]),
                  _const_spec(w_up.shape), _const_spec(w_dn.shape), _const_spec(w_pg.shape),
                  _const_spec(w_ple.shape), _const_spec((1, d)), _const_spec((1, d))],
        out_specs=tile(d),
        out_shape=jax.ShapeDtypeStruct((n, d), F32),
        compiler_params=pltpu.CompilerParams(
            dimension_semantics=("arbitrary",), vmem_limit_bytes=VMEM_LIMIT_BYTES),
        name="ffn",
    )(x1, p, w_up, w_dn, w_pg, w_ple, ln_g.reshape(1, d), ln_b.reshape(1, d))


def _pick_tile(n, candidates):
    for c in candidates:
        if n % c == 0:
            return c
    return n


def _layer(x_p, x_s, p_p, p_s, cache_kt, cache_vt, layer, state_conv, page_table, w, alpha):
    (w_in, sb_bias, conv_w, conv_b, conv_ln_g, conv_ln_b, w_conv_proj, w_att_proj, w_out,
     ln1_g, ln1_b, w_ffn_up, w_ffn_down, w_ple_gate, w_ple, ln2_g, ln2_b) = w
    bp, tp, d = x_p.shape
    bs, ts, _ = x_s.shape
    c = conv_w.shape[1]
    w_a = w_g = w_in.astype(BF16)
    w_cp, w_ap, w_o = (a.astype(BF16) for a in (w_conv_proj, w_att_proj, w_out))
    w_up, w_dn, w_pg, w_pl = (a.astype(BF16) for a in (w_ffn_up, w_ffn_down, w_ple_gate, w_ple))
    page = cache_kt.shape[4]

    tm_p = _pick_tile(tp, (512, 256, 128, 64, 32, 16, 8))
    cact_p, q_p, k_p, v_p, kb_p, vb_p, conv_p = _inproj_conv(
        x_p, w_a, jnp.zeros((bp, HIST_PAD, c), F32), conv_w, conv_b, conv_ln_g, conv_ln_b,
        bb=1, tm=tm_p, valid_last=tm_p)
    blk = _pick_tile(tp, (256, 128))
    o_p = _prompt_attention(sb_bias, q_p, kb_p, vb_p, blk=blk)

    ts_pad = -(-ts // 8) * 8
    x_s_pad = jnp.pad(x_s, ((0, 0), (0, ts_pad - ts), (0, 0)))
    past_pad = jnp.pad(state_conv, ((0, 0), (HIST_PAD - HIST, 0), (0, 0)))
    cact_s, q_s, k_s, v_s, kb_s, vb_s, conv_s = _inproj_conv(
        x_s_pad, w_a, past_pad, conv_w, conv_b, conv_ln_g, conv_ln_b,
        bb=_pick_tile(bs, (32, 16, 8, 4, 2, 1)), tm=ts_pad, valid_last=ts)
    knt = jnp.pad(jnp.swapaxes(kb_s, 1, 2), ((0, 0), (0, 0), (0, page - ts_pad)))
    vnt = jnp.pad(jnp.swapaxes(vb_s, 1, 2), ((0, 0), (0, 0), (0, page - ts_pad)))
    ppc = _pick_tile(page_table.shape[1], (32, 16, 8, 4, 2, 1))
    o_s = _sample_attention(page_table, q_s, knt, vnt, sb_bias, cache_kt, cache_vt, layer, ppc=ppc)

    def tail(x, cact, o, p):
        n = x.shape[0]
        tm = _pick_tile(n, (512, 256, 128, 64, 32, 16, 8))
        x1 = _merge(x, cact, o, w_g, w_cp, w_ap, w_o, ln1_g, ln1_b, tm=tm, alpha=alpha)
        return _ffn(x1, p, w_up, w_dn, w_pg, w_pl, ln2_g, ln2_b, tm=tm, alpha=alpha, chunk=256)

    y_p = tail(x_p.reshape(bp * tp, d), cact_p.reshape(bp * tp, c),
               o_p.reshape(bp * tp, ATT_WIDTH), p_p.reshape(bp * tp, -1)).reshape(bp, tp, d)
    y_s = tail(x_s.reshape(bs * ts, d), cact_s[:, :ts].reshape(bs * ts, c),
               o_s[:, :ts].reshape(bs * ts, ATT_WIDTH).astype(BF16), p_s.reshape(bs * ts, -1)).reshape(bs, ts, d)

    heads = lambda a, b, t: a[:, :t].reshape(b, t, N_HEADS, HEAD_DIM)
    return (y_p, y_s, heads(k_p, bp, tp), heads(v_p, bp, tp), conv_p,
            heads(k_s, bs, ts), heads(v_s, bs, ts), conv_s)


def kernel(x_prompt, x_sample, p_prompt, p_sample, cache_k, cache_v, state_conv, page_table, w_in, sb_bias, conv_w, conv_b, conv_ln_g, conv_ln_b, w_conv_proj, w_att_proj, w_out, ln1_g, ln1_b, w_ffn_up, w_ffn_down, w_ple_gate, w_ple, ln2_g, ln2_b):
    depth = w_in.shape[0]
    alpha = (2.0 * depth) ** 0.25
    y_p, y_s = x_prompt, x_sample
    outs = [[] for _ in range(6)]
    cache_kt = jnp.transpose(cache_k, (0, 1, 3, 4, 2))
    cache_vt = jnp.transpose(cache_v, (0, 1, 3, 4, 2))
    for i in range(depth):
        w = (w_in[i], sb_bias[i], conv_w[i], conv_b[i], conv_ln_g[i], conv_ln_b[i],
             w_conv_proj[i], w_att_proj[i], w_out[i], ln1_g[i], ln1_b[i], w_ffn_up[i],
             w_ffn_down[i], w_ple_gate[i], w_ple[i], ln2_g[i], ln2_b[i])
        y_p, y_s, k1, v1, c1, k2, v2, c2 = _layer(
            y_p, y_s, p_prompt[i], p_sample[i], cache_kt, cache_vt, i, state_conv[i],
            page_table, w, alpha)
        for lst, a in zip(outs, (k1, v1, c1, k2, v2, c2)):
            lst.append(a)
    return (y_p, y_s) + tuple(jnp.stack(lst) for lst in outs)
```

```python
import functools

import jax
import jax.numpy as jnp
from jax import lax
from jax.experimental import pallas as pl
from jax.experimental.pallas import tpu as pltpu

F32 = jnp.float32
BF16 = jnp.bfloat16

N_HEADS = 8
HEAD_DIM = 64
ATT_WIDTH = N_HEADS * HEAD_DIM
CONV_WIDTH = 31
HIST = CONV_WIDTH - 1
HIST_PAD = 32
LN_EPS = 1e-5
LANES = 128
SUBLANES = 8
HEADS_PER_LANE_TILE = LANES // HEAD_DIM
VMEM_LIMIT_BYTES = 56 * 1024 * 1024
NEG_BIG = -1e30
LOG2E = 1.4426950408889634


def _layer_norm(x, g, b):
    mu = jnp.mean(x, axis=-1, keepdims=True)
    xc = x - mu
    var = jnp.mean(xc * xc, axis=-1, keepdims=True)
    return xc * lax.rsqrt(var + LN_EPS) * g + b


def _softplus2(z2):
    neg_abs = lax.bitcast_convert_type(
        lax.bitcast_convert_type(z2, jnp.uint32) | jnp.uint32(0x80000000), F32)
    return jnp.maximum(z2, 0.0) + jnp.log2(1.0 + jnp.exp2(neg_abs))


def _split_bf16(x):
    hi = x.astype(BF16)
    lo = (x - hi.astype(F32)).astype(BF16)
    return hi, lo


def _tri_ge(n):
    j = lax.broadcasted_iota(jnp.int32, (n, n), 0)
    s = lax.broadcasted_iota(jnp.int32, (n, n), 1)
    return (j >= s).astype(BF16)


def _const_spec(shape):
    nd = len(shape)
    return pl.BlockSpec(shape, lambda *_: (0,) * nd, pipeline_mode=pl.Buffered(1))


def _conv_taps(ub, cw_ref, cb_ref, tm):
    acc = jnp.zeros((tm, ub.shape[1]), F32) + cb_ref[...]
    for rho in range(SUBLANES):
        taps = [kk for kk in range(CONV_WIDTH) if (HIST_PAD - HIST + kk) % SUBLANES == rho]
        ext = SUBLANES if rho else 0
        part = None
        for kk in taps:
            term = ub[pl.ds(HIST_PAD - HIST + kk - rho, tm + ext), :] * cw_ref[pl.ds(kk, 1), :]
            part = term if part is None else part + term
        acc = acc + part[rho:rho + tm, :]
    return acc


def _inproj_conv_body(x_ref, w_ref, past_ref, cw_ref, cb_ref, lg_ref, lb_ref,
                      cact_ref, q_ref, k_ref, v_ref, kb_ref, vb_ref, nc_ref, ubuf,
                      *, bb, tm, nt, valid_last, conv_ch):
    t = pl.program_id(1)
    d = x_ref.shape[2]

    @pl.when(t == 0)
    def _():
        ubuf[:, pl.ds(0, HIST_PAD), :] = past_ref[...]

    xb = x_ref[...].reshape(bb * tm, d).astype(BF16)
    h = jnp.dot(xb, w_ref[:, pl.ds(0, 2 * conv_ch + 3 * ATT_WIDTH)], preferred_element_type=F32)
    c = conv_ch
    u = h[:, :c] * jax.nn.sigmoid(h[:, c:2 * c])
    q = h[:, 2 * c:2 * c + ATT_WIDTH]
    k = h[:, 2 * c + ATT_WIDTH:2 * c + 2 * ATT_WIDTH]
    v = h[:, 2 * c + 2 * ATT_WIDTH:2 * c + 3 * ATT_WIDTH]
    per_seq = lambda a: a.reshape(bb, tm, a.shape[1])
    q_ref[...] = per_seq((q * (HEAD_DIM ** -0.5)).astype(BF16))
    k_ref[...] = per_seq(k)
    v_ref[...] = per_seq(v)
    kb_ref[...] = per_seq(k.astype(BF16))
    vb_ref[...] = per_seq(v.astype(BF16))

    ubuf[:, pl.ds(HIST_PAD, tm), :] = per_seq(u)
    acc = [_conv_taps(ubuf.at[s], cw_ref, cb_ref, tm) for s in range(bb)]
    acc = acc[0] if bb == 1 else jnp.concatenate(acc, axis=0)
    y = _layer_norm(acc, lg_ref[...], lb_ref[...])
    cact_ref[...] = per_seq((y * jax.nn.sigmoid(y)).astype(BF16))

    @pl.when(t == nt - 1)
    def _():
        nc_ref[...] = ubuf[:, pl.ds(HIST_PAD + valid_last - HIST, HIST), :]

    if nt > 1:
        ubuf[:, pl.ds(0, HIST_PAD), :] = ubuf[:, pl.ds(tm, HIST_PAD), :]


def _inproj_conv(x, w_a, past_pad, conv_w, conv_b, ln_g, ln_b, *, bb, tm, valid_last):
    b, t, d = x.shape
    c = conv_w.shape[1]
    nt = t // tm
    assert nt * tm == t and tm % SUBLANES == 0 and b % bb == 0
    body = functools.partial(_inproj_conv_body, bb=bb, tm=tm, nt=nt, valid_last=valid_last,
                             conv_ch=c)
    tile = lambda w: pl.BlockSpec((bb, tm, w), lambda i, j: (i, j, 0))
    per_b = lambda r, w: pl.BlockSpec((bb, r, w), lambda i, j: (i, 0, 0))
    return pl.pallas_call(
        body,
        grid=(b // bb, nt),
        in_specs=[tile(d), _const_spec(w_a.shape), per_b(HIST_PAD, c),
                  _const_spec(conv_w.shape), _const_spec((1, c)), _const_spec((1, c)),
                  _const_spec((1, c))],
        out_specs=[tile(c), tile(ATT_WIDTH), tile(ATT_WIDTH), tile(ATT_WIDTH), tile(ATT_WIDTH),
                   tile(ATT_WIDTH), per_b(HIST, c)],
        out_shape=[jax.ShapeDtypeStruct((b, t, c), BF16),
                   jax.ShapeDtypeStruct((b, t, ATT_WIDTH), BF16),
                   jax.ShapeDtypeStruct((b, t, ATT_WIDTH), F32),
                   jax.ShapeDtypeStruct((b, t, ATT_WIDTH), F32),
                   jax.ShapeDtypeStruct((b, t, ATT_WIDTH), BF16),
                   jax.ShapeDtypeStruct((b, t, ATT_WIDTH), BF16),
                   jax.ShapeDtypeStruct((b, HIST, c), F32)],
        scratch_shapes=[pltpu.VMEM((bb, HIST_PAD + tm, c), F32)],
        compiler_params=pltpu.CompilerParams(
            dimension_semantics=("arbitrary", "arbitrary"), vmem_limit_bytes=VMEM_LIMIT_BYTES),
        name="inproj_conv",
    )(x, w_a, past_pad, conv_w, conv_b.reshape(1, c), ln_g.reshape(1, c), ln_b.reshape(1, c))


def _prompt_attn_body(bias_ref, q_ref, k_ref, v_ref, o_ref, qm_ref, tri_ref, acc_ref, *, blk):
    qi = pl.program_id(1)
    row = lax.broadcasted_iota(jnp.int32, (blk, blk), 0)
    col = lax.broadcasted_iota(jnp.int32, (blk, blk), 1)
    lane = lax.broadcasted_iota(jnp.int32, (blk, LANES), 1)
    dims = (((1,), (1,)), ((), ()))
    tri_ref[...] = _tri_ge(blk)
    for h in range(N_HEADS):
        p, hh = divmod(h, HEADS_PER_LANE_TILE)
        qp = q_ref[0, :, pl.ds(p * LANES, LANES)]
        in_head = (lane >= hh * HEAD_DIM) & (lane < (hh + 1) * HEAD_DIM)
        qm_ref[h] = jnp.where(in_head, qp, jnp.zeros_like(qp))

    heads = range(N_HEADS)

    def key_blocks(lo_block, nb, carries, causal):
        keys = pl.ds(pl.multiple_of(lo_block * blk, blk), nb * blk)
        lanes = [pl.ds((h // HEADS_PER_LANE_TILE) * LANES, LANES) for h in heads]
        z2 = [(lax.dot_general(qm_ref[h], k_ref[0, keys, lanes[h]], dims,
                               preferred_element_type=F32) + bias_ref[h]) * LOG2E for h in heads]
        z2 = [[z[:, i * blk:(i + 1) * blk] for i in range(nb)] for z in z2]
        lhs = []
        for h in heads:
            for i in range(nb):
                sp = _softplus2(z2[h][i])
                if causal is not None and i == nb - 1:
                    sp = jnp.where(causal, sp, 0.0)
                lhs.append(sp.astype(BF16))
        local = [jnp.dot(x, tri_ref[...], preferred_element_type=F32) for x in lhs]
        a, new_carries = [], []
        for h in heads:
            carry = None if carries is None else carries[h]
            tiles = [None] * nb
            for i in reversed(range(nb)):
                cs = local[h * nb + i] if carry is None else local[h * nb + i] + carry
                carry = cs[:, 0:1]
                arg = z2[h][i] - cs
                if causal is not None and i == nb - 1:
                    arg = jnp.where(causal, arg, NEG_BIG)
                tiles[i] = jnp.exp2(arg).astype(BF16)
            a.append(tiles[0] if nb == 1 else jnp.concatenate(tiles, axis=1))
            new_carries.append(carry)
        for h in heads:
            pv = jnp.dot(a[h], v_ref[0, keys, lanes[h]], preferred_element_type=F32)
            if carries is None:
                acc_ref[h] = pv
            else:
                acc_ref[h] += pv
        return tuple(new_carries)

    odd = qi % 2
    carries = lax.cond(odd == 0,
                       lambda: key_blocks(qi, 1, None, col < row),
                       lambda: key_blocks(qi - 1, 2, None, col < row))
    top = qi - 1 - odd
    lax.fori_loop(0, qi // 2, lambda jj, c: key_blocks(top - 1 - 2 * jj, 2, c, None), carries)
    for p in range(N_HEADS // HEADS_PER_LANE_TILE):
        o_pair = jnp.where(lane < HEAD_DIM, acc_ref[HEADS_PER_LANE_TILE * p],
                           acc_ref[HEADS_PER_LANE_TILE * p + 1])
        o_ref[0, :, pl.ds(p * LANES, LANES)] = o_pair.astype(BF16)


def _prompt_attention(sb_bias, q, kb, vb, *, blk):
    b, t, w = q.shape
    nq = t // blk
    assert nq * blk == t and HEADS_PER_LANE_TILE == 2
    body = functools.partial(_prompt_attn_body, blk=blk)
    return pl.pallas_call(
        body,
        grid=(b, nq),
        in_specs=[pl.BlockSpec(memory_space=pltpu.SMEM),
                  pl.BlockSpec((1, blk, w), lambda i, j: (i, j, 0)),
                  pl.BlockSpec((1, t, w), lambda i, j: (i, 0, 0)),
                  pl.BlockSpec((1, t, w), lambda i, j: (i, 0, 0))],
        out_specs=pl.BlockSpec((1, blk, w), lambda i, j: (i, j, 0)),
        out_shape=jax.ShapeDtypeStruct((b, t, w), BF16),
        scratch_shapes=[pltpu.VMEM((N_HEADS, blk, LANES), BF16),
                        pltpu.VMEM((blk, blk), BF16),
                        pltpu.VMEM((N_HEADS, blk, LANES), F32)],
        compiler_params=pltpu.CompilerParams(
            dimension_semantics=("arbitrary", "arbitrary"), vmem_limit_bytes=VMEM_LIMIT_BYTES),
        name="prompt_attention",
    )(sb_bias, q, kb, vb)


def _sample_attn_body(pt_ref, q_ref, knt_ref, vnt_ref, bias_ref, *rest, ppc, page):
    k_refs = rest[:ppc]
    v_refs = rest[ppc:2 * ppc]
    o_ref = rest[2 * ppc]
    qh_ref, tri_ref, carry_ref, acc_ref = rest[2 * ppc + 1:]
    del pt_ref
    c = pl.program_id(1)
    nc = pl.num_programs(1)
    rows = q_ref.shape[1]
    dims_nt = (((1,), (1,)), ((), ()))
    heads = range(N_HEADS)

    def process(kt_heads, vt_heads, n, valid):
        z2 = [(jnp.dot(qh_ref[h].astype(BF16), kt_heads[h], preferred_element_type=F32)
               + bias_ref[h]) * LOG2E for h in heads]
        zs = jnp.concatenate([z2[h][:, p * page:(p + 1) * page] for h in heads for p in range(n)],
                             axis=0)
        sp = _softplus2(zs)
        if valid is not None:
            sp = jnp.where(valid, sp, 0.0)
        r = jnp.dot(jnp.concatenate(_split_bf16(sp), axis=1), tri_ref[...],
                    preferred_element_type=F32)
        local, total = r[:, :page], r[:, page:]
        a = []
        for h in heads:
            carry = carry_ref[h]
            tiles = []
            for p in range(n):
                rs = slice((h * n + p) * rows, (h * n + p + 1) * rows)
                arg = zs[rs] - (local[rs] + carry)
                if valid is not None:
                    arg = jnp.where(valid[rs], arg, NEG_BIG)
                tiles.append(jnp.exp2(arg))
                carry = carry + total[rs]
            carry_ref[h] = carry
            a.append(jnp.concatenate(tiles, axis=1).astype(BF16))
        for h in heads:
            acc_ref[h] += lax.dot_general(a[h], vt_heads[h], dims_nt, preferred_element_type=F32)

    @pl.when(c == 0)
    def _():
        tri = _tri_ge(page)
        both = jnp.concatenate([tri, jnp.ones_like(tri)], axis=1)
        tri_ref[pl.ds(0, page), :] = both
        tri_ref[pl.ds(page, page), :] = both
        for h in heads:
            qh_ref[h] = q_ref[0, :, pl.ds(h * HEAD_DIM, HEAD_DIM)].astype(F32)
        carry_ref[...] = jnp.zeros_like(carry_ref)
        acc_ref[...] = jnp.zeros_like(acc_ref)
        key = lax.broadcasted_iota(jnp.int32, (N_HEADS * rows, page), 1)
        qidx = lax.broadcasted_iota(jnp.int32, (N_HEADS * rows, page), 0) % rows
        process([knt_ref[0, pl.ds(h * HEAD_DIM, HEAD_DIM), :] for h in heads],
                [vnt_ref[0, pl.ds(h * HEAD_DIM, HEAD_DIM), :] for h in heads], 1, key < qidx)

    def head_slab(refs, h):
        return jnp.concatenate([r[h] for r in refs], axis=1).astype(BF16)

    process([head_slab(k_refs, h) for h in heads], [head_slab(v_refs, h) for h in heads], ppc, None)

    @pl.when(c == nc - 1)
    def _():
        for h in heads:
            o_ref[0, :, pl.ds(h * HEAD_DIM, HEAD_DIM)] = acc_ref[h]


def _sample_attention(page_table, q, knt, vnt, sb_bias, cache_kt, cache_vt, layer, *, ppc):
    b, rows, _ = q.shape
    n_pages = page_table.shape[1]
    page = cache_kt.shape[4]
    nchunks = n_pages // ppc
    assert nchunks * ppc == n_pages and rows % 8 == 0 and page == LANES
    body = functools.partial(_sample_attn_body, ppc=ppc, page=page)

    def page_spec(p):
        return pl.BlockSpec(
            (None, None, N_HEADS, HEAD_DIM, page),
            lambda i, j, pt: (layer, pt[i, n_pages - 1 - (j * ppc + p)], 0, 0, 0))

    per_b = lambda r, w: pl.BlockSpec((1, r, w), lambda i, j, pt: (i, 0, 0))
    grid_spec = pltpu.PrefetchScalarGridSpec(
        num_scalar_prefetch=1,
        grid=(b, nchunks),
        in_specs=[per_b(rows, ATT_WIDTH), per_b(ATT_WIDTH, page), per_b(ATT_WIDTH, page),
                  pl.BlockSpec(memory_space=pltpu.SMEM)]
                 + [page_spec(p) for p in range(ppc)] * 2,
        out_specs=per_b(rows, ATT_WIDTH),
        scratch_shapes=[pltpu.VMEM((N_HEADS, rows, HEAD_DIM), F32),
                        pltpu.VMEM((2 * page, 2 * page), BF16),
                        pltpu.VMEM((N_HEADS, rows, page), F32),
                        pltpu.VMEM((N_HEADS, rows, HEAD_DIM), F32)],
    )
    return pl.pallas_call(
        body,
        grid_spec=grid_spec,
        out_shape=jax.ShapeDtypeStruct(q.shape, F32),
        compiler_params=pltpu.CompilerParams(
            dimension_semantics=("arbitrary", "arbitrary"), vmem_limit_bytes=VMEM_LIMIT_BYTES),
        name="sample_attention",
    )(page_table, q, knt, vnt, sb_bias, *([cache_kt] * ppc), *([cache_vt] * ppc))


def _merge_body(x_ref, cact_ref, o_ref, wg_ref, wcp_ref, wap_ref, wout_ref, g_ref, b_ref,
                x1_ref, *, alpha, d, n_mix, parts):
    rows = x_ref.shape[0] // parts
    groups = [pl.ds(i * rows, rows) for i in range(parts)]
    xs = [x_ref[g, :] for g in groups]
    gates = [jnp.dot(x.astype(BF16), wg_ref[:, pl.ds(n_mix, 2 * d)], preferred_element_type=F32)
             for x in xs]
    conv_out = [jnp.dot(cact_ref[g, :], wcp_ref[...], preferred_element_type=F32) for g in groups]
    att_out = [jnp.dot(o_ref[g, :], wap_ref[...], preferred_element_type=F32) for g in groups]
    m = [jax.nn.sigmoid(gt[:, :d]) * co + jax.nn.sigmoid(gt[:, d:]) * ao
         for gt, co, ao in zip(gates, conv_out, att_out)]
    mixed = [jnp.dot(mi.astype(BF16), wout_ref[...], preferred_element_type=F32) for mi in m]
    for g, x, mx in zip(groups, xs, mixed):
        x1_ref[g, :] = _layer_norm(alpha * x + mx, g_ref[...], b_ref[...])


def _merge(x, cact, o, w_g, w_cp, w_ap, w_out, ln_g, ln_b, *, tm, alpha):
    n, d = x.shape
    assert n % tm == 0
    parts = 2 if tm % (2 * 16) == 0 else 1
    body = functools.partial(_merge_body, alpha=alpha, d=d, n_mix=w_g.shape[1] - 2 * d,
                             parts=parts)
    tile = lambda w: pl.BlockSpec((tm, w), lambda i: (i, 0))
    return pl.pallas_call(
        body,
        grid=(n // tm,),
        in_specs=[tile(d), tile(cact.shape[1]), tile(o.shape[1]),
                  _const_spec(w_g.shape), _const_spec(w_cp.shape), _const_spec(w_ap.shape),
                  _const_spec(w_out.shape), _const_spec((1, d)), _const_spec((1, d))],
        out_specs=tile(d),
        out_shape=jax.ShapeDtypeStruct((n, d), F32),
        compiler_params=pltpu.CompilerParams(
            dimension_semantics=("arbitrary",), vmem_limit_bytes=VMEM_LIMIT_BYTES),
        name="merge",
    )(x, cact, o, w_g, w_cp, w_ap, w_out, ln_g.reshape(1, d), ln_b.reshape(1, d))


def _ffn_body(x1_ref, p_ref, wup_ref, wdn_ref, wpg_ref, wple_ref, g_ref, b_ref, y_ref,
              *, alpha, hidden, chunk):
    x1 = x1_ref[...]
    x1b = x1.astype(BF16)
    ple = (jax.nn.sigmoid(jnp.dot(x1b, wpg_ref[...], preferred_element_type=F32))
           * jnp.dot(p_ref[...].astype(BF16), wple_ref[...], preferred_element_type=F32))
    acc = alpha * x1 + ple
    for c0 in range(0, hidden, chunk):
        gate = jnp.dot(x1b, wup_ref[:, pl.ds(c0, chunk)], preferred_element_type=F32)
        up = jnp.dot(x1b, wup_ref[:, pl.ds(hidden + c0, chunk)], preferred_element_type=F32)
        act = (gate * jax.nn.sigmoid(gate) * up).astype(BF16)
        acc = acc + jnp.dot(act, wdn_ref[pl.ds(c0, chunk), :], preferred_element_type=F32)
    y_ref[...] = _layer_norm(acc, g_ref[...], b_ref[...])


def _ffn(x1, p, w_up, w_dn, w_pg, w_ple, ln_g, ln_b, *, tm, alpha, chunk):
    n, d = x1.shape
    hidden = w_dn.shape[0]
    assert n % tm == 0 and hidden % chunk == 0
    body = functools.partial(_ffn_body, alpha=alpha, hidden=hidden, chunk=chunk)
    tile = lambda w: pl.BlockSpec((tm, w), lambda i: (i, 0))
    return pl.pallas_call(
        body,
        grid=(n // tm,),
        in_specs=[tile(d), tile(p.shape[1]),
                  _const_spec(w_up.shape), _const_spec(w_dn.shape), _const_spec(w_pg.shape),
                  _const_spec(w_ple.shape), _const_spec((1, d)), _const_spec((1, d))],
        out_specs=tile(d),
        out_shape=jax.ShapeDtypeStruct((n, d), F32),
        compiler_params=pltpu.CompilerParams(
            dimension_semantics=("arbitrary",), vmem_limit_bytes=VMEM_LIMIT_BYTES),
        name="ffn",
    )(x1, p, w_up, w_dn, w_pg, w_ple, ln_g.reshape(1, d), ln_b.reshape(1, d))


def _pick_tile(n, candidates):
    for c in candidates:
        if n % c == 0:
            return c
    return n


def _layer(x_p, x_s, p_p, p_s, cache_kt, cache_vt, layer, state_conv, page_table, w, alpha):
    (w_in, sb_bias, conv_w, conv_b, conv_ln_g, conv_ln_b, w_conv_proj, w_att_proj, w_out,
     ln1_g, ln1_b, w_ffn_up, w_ffn_down, w_ple_gate, w_ple, ln2_g, ln2_b) = w
    bp, tp, d = x_p.shape
    bs, ts, _ = x_s.shape
    c = conv_w.shape[1]
    w_a = w_g = w_in.astype(BF16)
    w_cp, w_ap, w_o = (a.astype(BF16) for a in (w_conv_proj, w_att_proj, w_out))
    w_up, w_dn, w_pg, w_pl = (a.astype(BF16) for a in (w_ffn_up, w_ffn_down, w_ple_gate, w_ple))
    page = cache_kt.shape[4]

    tm_p = _pick_tile(tp, (1024, 512, 256, 128, 64, 32, 16, 8))
    cact_p, q_p, k_p, v_p, kb_p, vb_p, conv_p = _inproj_conv(
        x_p, w_a, jnp.zeros((bp, HIST_PAD, c), F32), conv_w, conv_b, conv_ln_g, conv_ln_b,
        bb=1, tm=tm_p, valid_last=tm_p)
    blk = _pick_tile(tp, (256, 128))
    o_p = _prompt_attention(sb_bias, q_p, kb_p, vb_p, blk=blk)

    ts_pad = -(-ts // 8) * 8
    x_s_pad = jnp.pad(x_s, ((0, 0), (0, ts_pad - ts), (0, 0)))
    past_pad = jnp.pad(state_conv, ((0, 0), (HIST_PAD - HIST, 0), (0, 0)))
    cact_s, q_s, k_s, v_s, kb_s, vb_s, conv_s = _inproj_conv(
        x_s_pad, w_a, past_pad, conv_w, conv_b, conv_ln_g, conv_ln_b,
        bb=_pick_tile(bs, (32, 16, 8, 4, 2, 1)), tm=ts_pad, valid_last=ts)
    knt = jnp.pad(jnp.swapaxes(kb_s, 1, 2), ((0, 0), (0, 0), (0, page - ts_pad)))
    vnt = jnp.pad(jnp.swapaxes(vb_s, 1, 2), ((0, 0), (0, 0), (0, page - ts_pad)))
    ppc = _pick_tile(page_table.shape[1], (32, 16, 8, 4, 2, 1))
    o_s = _sample_attention(page_table, q_s, knt, vnt, sb_bias, cache_kt, cache_vt, layer, ppc=ppc)

    def tail(x, cact, o, p):
        n = x.shape[0]
        tm = _pick_tile(n, (512, 256, 128, 64, 32, 16, 8))
        tm_merge = _pick_tile(n, (1024, tm))
        x1 = _merge(x, cact, o, w_g, w_cp, w_ap, w_o, ln1_g, ln1_b, tm=tm_merge, alpha=alpha)
        return _ffn(x1, p, w_up, w_dn, w_pg, w_pl, ln2_g, ln2_b, tm=tm, alpha=alpha, chunk=256)

    y_p = tail(x_p.reshape(bp * tp, d), cact_p.reshape(bp * tp, c),
               o_p.reshape(bp * tp, ATT_WIDTH), p_p.reshape(bp * tp, -1)).reshape(bp, tp, d)
    y_s = tail(x_s.reshape(bs * ts, d), cact_s[:, :ts].reshape(bs * ts, c),
               o_s[:, :ts].reshape(bs * ts, ATT_WIDTH).astype(BF16), p_s.reshape(bs * ts, -1)).reshape(bs, ts, d)

    heads = lambda a, b, t: a[:, :t].reshape(b, t, N_HEADS, HEAD_DIM)
    return (y_p, y_s, heads(k_p, bp, tp), heads(v_p, bp, tp), conv_p,
            heads(k_s, bs, ts), heads(v_s, bs, ts), conv_s)


def kernel(x_prompt, x_sample, p_prompt, p_sample, cache_k, cache_v, state_conv, page_table, w_in, sb_bias, conv_w, conv_b, conv_ln_g, conv_ln_b, w_conv_proj, w_att_proj, w_out, ln1_g, ln1_b, w_ffn_up, w_ffn_down, w_ple_gate, w_ple, ln2_g, ln2_b):
    depth = w_in.shape[0]
    alpha = (2.0 * depth) ** 0.25
    y_p, y_s = x_prompt, x_sample
    outs = [[] for _ in range(6)]
    cache_kt = jnp.transpose(cache_k, (0, 1, 3, 4, 2))
    cache_vt = jnp.transpose(cache_v, (0, 1, 3, 4, 2))
    for i in range(depth):
        w = (w_in[i], sb_bias[i], conv_w[i], conv_b[i], conv_ln_g[i], conv_ln_b[i],
             w_conv_proj[i], w_att_proj[i], w_out[i], ln1_g[i], ln1_b[i], w_ffn_up[i],
             w_ffn_down[i], w_ple_gate[i], w_ple[i], ln2_g[i], ln2_b[i])
        y_p, y_s, k1, v1, c1, k2, v2, c2 = _layer(
            y_p, y_s, p_prompt[i], p_sample[i], cache_kt, cache_vt, i, state_conv[i],
            page_table, w, alpha)
        for lst, a in zip(outs, (k1, v1, c1, k2, v2, c2)):
            lst.append(a)
    return (y_p, y_s) + tuple(jnp.stack(lst) for lst in outs)
```
